```python
import math
import jax, jax.numpy as jnp
from jax import lax
import numpy as np

D_MODEL = 2048
BATCH = 8
SEQ = 2048
DEPTH = 2

BLOCK = 128
EPS = 1e-6
A_HEADS = 8
A_HEAD_DIM = 128
A_WIDTH = A_HEADS * A_HEAD_DIM
POOL_WINDOWS = (2, 4, 8, 16)
B_GROUP_DIM = 256
B_WIDTH = len(POOL_WINDOWS) * B_GROUP_DIM
C_PATTERNS = ((128, 1), (512, 4), (2048, 16))
C_GROUPS = len(C_PATTERNS)
C_HEADS_PER_GROUP = 4
C_HEAD_DIM = 128
C_HEADS = C_GROUPS * C_HEADS_PER_GROUP
C_QKV_WIDTH = C_HEADS * C_HEAD_DIM
C_OUT_WIDTH = C_HEADS_PER_GROUP * C_HEAD_DIM
N_BRANCHES = 3
IN_WIDTH = 2 * A_WIDTH + B_WIDTH + 3 * C_QKV_WIDTH + N_BRANCHES * D_MODEL
REL_BUCKETS = 32
REL_MAX_DISTANCE = 2048
PEER_HEADS = 8
PEER_KEYS = 128
PEER_N_EXPERTS = PEER_KEYS * PEER_KEYS
PEER_KEY_DIM = 256
PEER_TOPK = 16
PEER_TOKEN_BLOCK = 128

kernel_name = "hybrid_gmlp_pool_dilated_peer"


def rmsnorm(x, g):
    xf = x.astype(jnp.float32)
    y = xf * lax.rsqrt(jnp.mean(xf * xf, axis=-1, keepdims=True) + EPS)
    return (y * g.astype(jnp.float32)).astype(x.dtype)


def rel_bucket(n):
    max_exact = REL_BUCKETS // 2
    n = np.asarray(n)
    nl = np.maximum(n, max_exact).astype(np.float64)
    large = max_exact + (np.log(nl / max_exact) / math.log(REL_MAX_DISTANCE / max_exact)
                         * (REL_BUCKETS - max_exact)).astype(np.int32)
    large = np.minimum(large, REL_BUCKETS - 1)
    return np.where(n < max_exact, n, large).astype(np.int32)


def chunked_spatial_gating(u, v, a_norm_g, w_s, b_s):
    B, S, _ = u.shape
    u = jax.nn.gelu(u, approximate=False)
    v = rmsnorm(jax.nn.gelu(v, approximate=False), a_norm_g)
    vc = v.reshape(B, S // BLOCK, BLOCK, A_HEADS, A_HEAD_DIM)
    w_causal = jnp.where(jnp.tril(jnp.ones((BLOCK, BLOCK), dtype=bool))[None], w_s, 0)
    s = jnp.einsum('hts,bcshd->bcthd', w_causal, vc) + b_s.T[None, None, :, :, None]
    return u * s.reshape(B, S, A_WIDTH).astype(u.dtype)


def multiscale_pool(p, w_pool, pool_scale):
    B, S, _ = p.shape
    pf = p.astype(jnp.float32)
    cs = jnp.concatenate([jnp.zeros((B, 1, B_WIDTH), jnp.float32), jnp.cumsum(pf, axis=1)], axis=1)
    end = jnp.arange(1, S + 1)
    diffs = []
    for g, w in enumerate(POOL_WINDOWS):
        lo, hi = g * B_GROUP_DIM, (g + 1) * B_GROUP_DIM
        start = jnp.maximum(end - w, 0)
        csg = cs[:, :, lo:hi]
        window_sum = csg[:, 1:] - jnp.take(csg, start, axis=1)
        mean = window_sum / (end - start).astype(jnp.float32)[None, :, None]
        diffs.append(mean - pf[:, :, lo:hi])
    d = jnp.stack(diffs, axis=2).astype(p.dtype)
    y = jnp.einsum('bsgc,gce->bsge', d, w_pool)
    return y.reshape(B, S, B_WIDTH) * pool_scale


def dilated_group_attention(q, k, v, bias_hj, dilation, n_off):
    B, S, H, Dh = q.shape
    L = S // dilation
    nb = -(-L // BLOCK)
    Lp = nb * BLOCK

    def to_sub(t):
        t = t.reshape(B, L, dilation, H, Dh).transpose(0, 2, 1, 3, 4).reshape(B * dilation, L, H, Dh)
        t = jnp.pad(t, ((0, 0), (0, Lp - L), (0, 0), (0, 0)))
        return t.reshape(B * dilation, nb, BLOCK, H, Dh)

    def with_prev(t):
        prev = jnp.pad(t, ((0, 0), (1, 0), (0, 0), (0, 0), (0, 0)))[:, :-1]
        return jnp.concatenate([prev, t], axis=2)

    qb = to_sub(q)
    kk = with_prev(to_sub(k))
    vv = with_prev(to_sub(v)).astype(jnp.float32)
    logits = jnp.einsum('bnqhd,bnkhd->bnhqk', qb, kk, preferred_element_type=jnp.float32) * (Dh ** -0.5)
    qi = np.arange(BLOCK)[:, None]
    ki = np.arange(2 * BLOCK)[None, :]
    off = BLOCK + qi - ki
    off_ok = (off >= 0) & (off <= n_off)
    key_ok = (np.arange(nb)[:, None] * BLOCK - BLOCK + np.arange(2 * BLOCK)[None, :]) >= 0
    mask = off_ok[None] & key_ok[:, None, :]
    bias = bias_hj[:, np.clip(off, 0, n_off)].astype(jnp.float32)
    logits = jnp.where(mask[None, :, None], logits + bias[None, None], jnp.float32(-1e30))
    m = jnp.max(logits, axis=-1, keepdims=True)
    e = jnp.exp(logits - m)
    s = jnp.sum(e, axis=-1, keepdims=True)
    lse = (m + jnp.log(s))[..., 0].transpose(0, 1, 3, 2)
    o = jnp.einsum('bnhqk,bnkhd->bnqhd', e, vv) / s[..., 0].transpose(0, 1, 3, 2)[..., None]

    def from_sub(t):
        t = t.reshape(B * dilation, Lp, *t.shape[3:])[:, :L]
        t = t.reshape(B, dilation, L, *t.shape[2:])
        return jnp.swapaxes(t, 1, 2).reshape(B, S, *t.shape[3:])

    return from_sub(o), from_sub(lse)


def dilated_mixture_attention(q, k, v, rel_table):
    outs, lses = [], []
    for g, (window, dilation) in enumerate(C_PATTERNS):
        n_off = window // dilation
        buckets = rel_bucket(np.arange(n_off + 1) * dilation)
        bias_hj = rel_table[buckets][:, g * C_HEADS_PER_GROUP:(g + 1) * C_HEADS_PER_GROUP].T
        o, lse = dilated_group_attention(q[:, :, g], k[:, :, g], v[:, :, g], bias_hj, dilation, n_off)
        outs.append(o)
        lses.append(lse)
    o = jnp.stack(outs, axis=2)
    w = jax.nn.softmax(jnp.stack(lses, axis=2), axis=2)
    return jnp.sum(w[..., None] * o, axis=2)


def peer_ffn(x, w_q, sub_keys_1, sub_keys_2, expert_u, expert_v):
    B, S, D = x.shape
    T = B * S
    xt = x.reshape(T, D)
    q = (xt @ w_q).astype(jnp.float32).reshape(T, PEER_HEADS, 2, PEER_KEY_DIM // 2)
    s1 = jnp.einsum('thc,kc->thk', q[:, :, 0], sub_keys_1.astype(jnp.float32))
    s2 = jnp.einsum('thc,kc->thk', q[:, :, 1], sub_keys_2.astype(jnp.float32))
    v1, i1 = lax.top_k(s1, PEER_TOPK)
    v2, i2 = lax.top_k(s2, PEER_TOPK)
    cand = (v1[..., :, None] + v2[..., None, :]).reshape(T, PEER_HEADS, PEER_TOPK * PEER_TOPK)
    vs, pos = lax.top_k(cand, PEER_TOPK)
    e1 = jnp.take_along_axis(i1, pos // PEER_TOPK, axis=-1)
    e2 = jnp.take_along_axis(i2, pos % PEER_TOPK, axis=-1)
    experts = (e1 * PEER_KEYS + e2).reshape(T, PEER_HEADS * PEER_TOPK)
    gates = jax.nn.softmax(vs, axis=-1).reshape(T, PEER_HEADS * PEER_TOPK)
    nblk = T // PEER_TOKEN_BLOCK

    def token_block(args):
        xb, eb, gb = args
        a = jnp.einsum('td,tkd->tk', xb, expert_u[eb], preferred_element_type=jnp.float32)
        h = (jax.nn.gelu(a, approximate=False) * gb).astype(xb.dtype)
        return jnp.einsum('tk,tkd->td', h, expert_v[eb])

    y = lax.map(token_block, (xt.reshape(nblk, PEER_TOKEN_BLOCK, D),
                              experts.reshape(nblk, PEER_TOKEN_BLOCK, -1),
                              gates.reshape(nblk, PEER_TOKEN_BLOCK, -1)))
    return y.reshape(B, S, D).astype(x.dtype)


def hybrid_layer(x, rel_table, norm_mix_g, w_in, a_norm_g, a_w_s, a_b_s, b_w_pool, b_scale,
                 w_br_a, w_br_b, w_br_c, w_out, norm_ffn_g, peer_w_q, peer_k1, peer_k2, peer_u, peer_v):
    B, S, D = x.shape
    h = rmsnorm(x, norm_mix_g)
    p = h @ w_in
    cuts = np.cumsum([A_WIDTH, A_WIDTH, B_WIDTH, C_QKV_WIDTH, C_QKV_WIDTH, C_QKV_WIDTH]).tolist()
    a_u, a_v, b_in, c_q, c_k, c_v, gate_logits = jnp.split(p, cuts, axis=-1)
    y_a = chunked_spatial_gating(a_u, a_v, a_norm_g, a_w_s, a_b_s)
    y_b = multiscale_pool(b_in, b_w_pool, b_scale)
    shp = (B, S, C_GROUPS, C_HEADS_PER_GROUP, C_HEAD_DIM)
    y_c = dilated_mixture_attention(c_q.reshape(shp), c_k.reshape(shp), c_v.reshape(shp), rel_table)
    y_c = y_c.reshape(B, S, C_OUT_WIDTH).astype(x.dtype)
    gates = jax.nn.sigmoid(gate_logits.astype(jnp.float32)).astype(x.dtype).reshape(B, S, N_BRANCHES, D)
    merged = (gates[:, :, 0] * (y_a @ w_br_a) + gates[:, :, 1] * (y_b @ w_br_b)
              + gates[:, :, 2] * (y_c @ w_br_c))
    x = x + merged @ w_out
    x = x + peer_ffn(rmsnorm(x, norm_ffn_g), peer_w_q, peer_k1, peer_k2, peer_u, peer_v)
    return x


def setup_inputs(seed: int = 0) -> dict:
    key = jax.random.key(seed)
    ks = jax.random.split(key, 24)
    nrm = lambda k, shape, scale: jax.random.normal(k, shape, jnp.float32) * scale
    gain = lambda k, shape: 1.0 + 0.01 * jax.random.normal(k, shape, jnp.float32)
    D = D_MODEL
    return {
        "x": nrm(ks[0], (BATCH, SEQ, D), 1.0),
        "rel_bias_table": nrm(ks[1], (REL_BUCKETS, C_HEADS), 0.5),
        "norm_mix_g": gain(ks[2], (DEPTH, D)),
        "w_in": nrm(ks[3], (DEPTH, D, IN_WIDTH), D ** -0.5),
        "a_norm_g": gain(ks[4], (DEPTH, A_WIDTH)),
        "a_w_s": nrm(ks[5], (DEPTH, A_HEADS, BLOCK, BLOCK), BLOCK ** -0.5),
        "a_b_s": gain(ks[6], (DEPTH, A_HEADS, BLOCK)),
        "b_w_pool": nrm(ks[7], (DEPTH, len(POOL_WINDOWS), B_GROUP_DIM, B_GROUP_DIM), B_GROUP_DIM ** -0.5),
        "b_scale": 1.0 + nrm(ks[8], (DEPTH, B_WIDTH), 0.1),
        "w_br_a": nrm(ks[9], (DEPTH, A_WIDTH, D), A_WIDTH ** -0.5),
        "w_br_b": nrm(ks[10], (DEPTH, B_WIDTH, D), B_WIDTH ** -0.5),
        "w_br_c": nrm(ks[11], (DEPTH, C_OUT_WIDTH, D), C_OUT_WIDTH ** -0.5),
        "w_out": nrm(ks[12], (DEPTH, D, D), D ** -0.5),
        "norm_ffn_g": gain(ks[13], (DEPTH, D)),
        "peer_w_q": nrm(ks[14], (DEPTH, D, PEER_HEADS * PEER_KEY_DIM), D ** -0.5),
        "peer_sub_keys_1": nrm(ks[15], (DEPTH, PEER_KEYS, PEER_KEY_DIM // 2), (PEER_KEY_DIM // 2) ** -0.5),
        "peer_sub_keys_2": nrm(ks[16], (DEPTH, PEER_KEYS, PEER_KEY_DIM // 2), (PEER_KEY_DIM // 2) ** -0.5),
        "peer_u": nrm(ks[17], (DEPTH, PEER_N_EXPERTS, D), D ** -0.5),
        "peer_v": nrm(ks[18], (DEPTH, PEER_N_EXPERTS, D), PEER_HEADS ** -0.5),
        "final_norm_g": gain(ks[19], (D,)),
    }


def reference(x, rel_bias_table, norm_mix_g, w_in, a_norm_g, a_w_s, a_b_s, b_w_pool, b_scale,
              w_br_a, w_br_b, w_br_c, w_out, norm_ffn_g, peer_w_q, peer_sub_keys_1, peer_sub_keys_2,
              peer_u, peer_v, final_norm_g):
    for l in range(DEPTH):
        x = hybrid_layer(x, rel_bias_table, norm_mix_g[l], w_in[l], a_norm_g[l], a_w_s[l], a_b_s[l],
                         b_w_pool[l], b_scale[l], w_br_a[l], w_br_b[l], w_br_c[l], w_out[l],
                         norm_ffn_g[l], peer_w_q[l], peer_sub_keys_1[l], peer_sub_keys_2[l],
                         peer_u[l], peer_v[l])
    return rmsnorm(x, final_norm_g)
```

```python
import functools
import math

import numpy as np
import jax
import jax.numpy as jnp
from jax import lax
from jax.experimental import pallas as pl
from jax.experimental.pallas import tpu as pltpu

D_MODEL = 2048
DEPTH = 2
BLOCK = 128
EPS = 1e-6
A_HEADS = 8
A_HEAD_DIM = 128
A_WIDTH = A_HEADS * A_HEAD_DIM
POOL_WINDOWS = (2, 4, 8, 16)
B_GROUP_DIM = 256
B_WIDTH = len(POOL_WINDOWS) * B_GROUP_DIM
C_PATTERNS = ((128, 1), (512, 4), (2048, 16))
C_GROUPS = len(C_PATTERNS)
C_HEADS_PER_GROUP = 4
C_HEAD_DIM = 128
C_HEADS = C_GROUPS * C_HEADS_PER_GROUP
C_QKV_WIDTH = C_HEADS * C_HEAD_DIM
C_OUT_WIDTH = C_HEADS_PER_GROUP * C_HEAD_DIM
N_BRANCHES = 3
GATE_WIDTH = N_BRANCHES * D_MODEL
IN_WIDTH = 2 * A_WIDTH + B_WIDTH + 3 * C_QKV_WIDTH + GATE_WIDTH
REL_BUCKETS = 32
REL_MAX_DISTANCE = 2048
PEER_HEADS = 8
PEER_KEYS = 128
PEER_N_EXPERTS = PEER_KEYS * PEER_KEYS
PEER_KEY_DIM = 256
PEER_TOPK = 16

COL_GATE = 0
COL_AU = GATE_WIDTH
COL_AV = COL_AU + A_WIDTH
COL_B = COL_AV + A_WIDTH
COL_Q = COL_B + B_WIDTH
COL_K = COL_Q + C_QKV_WIDTH
COL_V = COL_K + C_QKV_WIDTH

NEG_MASK = -1e30
BF16 = jnp.bfloat16
F32 = jnp.float32

VMEM_LIMIT_BYTES = 56 * 1024 * 1024


def _params(n_axes):
    return pltpu.CompilerParams(
        dimension_semantics=("arbitrary",) * n_axes, vmem_limit_bytes=VMEM_LIMIT_BYTES)


def _rms(x, g):
    return x * lax.rsqrt(jnp.mean(x * x, axis=-1, keepdims=True) + EPS) * g


def _gelu(x):
    return 0.5 * x * (1.0 + lax.erf(x * (1.0 / math.sqrt(2.0))))


def _resident(shape):
    nd = len(shape)
    return pl.BlockSpec(shape, lambda *_: (0,) * nd, pipeline_mode=pl.Buffered(1))


def _norm_kernel(x_ref, g_ref, o_ref):
    o_ref[...] = _rms(x_ref[...], g_ref[...]).astype(o_ref.dtype)


def rmsnorm_rows(x, g, tm=512):
    t, d = x.shape
    return pl.pallas_call(
        _norm_kernel,
        grid=(t // tm,),
        in_specs=[pl.BlockSpec((tm, d), lambda i: (i, 0)), _resident((1, d))],
        out_specs=pl.BlockSpec((tm, d), lambda i: (i, 0)),
        out_shape=jax.ShapeDtypeStruct((t, d), BF16),
        compiler_params=_params(1),
        name="rmsnorm_rows",
    )(x, g.reshape(1, d))


def _matmul_kernel(h_ref, w_ref, o_ref):
    o_ref[...] = jnp.dot(h_ref[...], w_ref[...], preferred_element_type=F32).astype(o_ref.dtype)


def in_proj(h, w, tm=1024, tn=1536):
    t, d = h.shape
    n = w.shape[1]
    return pl.pallas_call(
        _matmul_kernel,
        grid=(n // tn, t // tm),
        in_specs=[pl.BlockSpec((tm, d), lambda j, i: (i, 0)),
                  pl.BlockSpec((d, tn), lambda j, i: (0, j))],
        out_specs=pl.BlockSpec((tm, tn), lambda j, i: (i, j)),
        out_shape=jax.ShapeDtypeStruct((t, n), BF16),
        compiler_params=_params(2),
        name="in_proj",
    )(h, w)


def _mixer_a_kernel(au_ref, av_ref, g_ref, ws_ref, bs_ref, o_ref, *, chunks):
    u = _gelu(au_ref[...].astype(F32))
    v = _rms(_gelu(av_ref[...].astype(F32)), g_ref[...]).astype(BF16)
    row = lax.broadcasted_iota(jnp.int32, (BLOCK, BLOCK), 0)
    col = lax.broadcasted_iota(jnp.int32, (BLOCK, BLOCK), 1)
    causal = col <= row
    for hd in range(A_HEADS):
        w = jnp.where(causal, ws_ref[hd], 0.0).astype(BF16)
        bias = bs_ref[hd]
        cs = slice(hd * A_HEAD_DIM, (hd + 1) * A_HEAD_DIM)
        for c in range(chunks):
            rs = slice(c * BLOCK, (c + 1) * BLOCK)
            s = jnp.dot(w, v[rs, cs], preferred_element_type=F32) + bias
            o_ref[rs, cs] = (u[rs, cs] * s).astype(o_ref.dtype)


def mixer_a(p, a_norm_g, a_w_s, a_b_s, chunks=4):
    t = p.shape[0]
    tm = chunks * BLOCK
    bias = jnp.broadcast_to(a_b_s[:, :, None], (A_HEADS, BLOCK, A_HEAD_DIM))
    return pl.pallas_call(
        functools.partial(_mixer_a_kernel, chunks=chunks),
        grid=(t // tm,),
        in_specs=[pl.BlockSpec((tm, A_WIDTH), lambda i: (i, COL_AU // A_WIDTH)),
                  pl.BlockSpec((tm, A_WIDTH), lambda i: (i, COL_AV // A_WIDTH)),
                  _resident((1, A_WIDTH)),
                  _resident((A_HEADS, BLOCK, BLOCK)),
                  _resident((A_HEADS, BLOCK, A_HEAD_DIM))],
        out_specs=pl.BlockSpec((tm, A_WIDTH), lambda i: (i, 0)),
        out_shape=jax.ShapeDtypeStruct((t, A_WIDTH), BF16),
        compiler_params=_params(1),
        name="mixer_a",
    )(p, p, a_norm_g.reshape(1, A_WIDTH), a_w_s, bias)


def _pool_band_matrices():
    t = np.arange(BLOCK)[:, None]
    k = np.arange(2 * BLOCK)[None, :]
    back = BLOCK + t - k
    return np.stack([((back >= 0) & (back < w)) for w in POOL_WINDOWS]).astype(np.float32)


def _mixer_b_kernel(prev_ref, cur_ref, band_ref, wp_ref, sc_ref, o_ref, *, chunks, tiles_per_seq):
    first = (pl.program_id(0) % tiles_per_seq) == 0
    prev = jnp.where(first, jnp.zeros_like(prev_ref[...]), prev_ref[...])
    tile_pos = (pl.program_id(0) % tiles_per_seq) * (chunks * BLOCK)
    for c in range(chunks):
        rs = slice(c * BLOCK, (c + 1) * BLOCK)
        before = prev if c == 0 else cur_ref[(c - 1) * BLOCK:c * BLOCK, :]
        cur = cur_ref[rs, :]
        both = jnp.concatenate([before, cur], axis=0)
        pos = tile_pos + c * BLOCK + lax.broadcasted_iota(jnp.int32, (BLOCK, B_GROUP_DIM), 0)
        for g, w in enumerate(POOL_WINDOWS):
            cs = slice(g * B_GROUP_DIM, (g + 1) * B_GROUP_DIM)
            wsum = jnp.dot(band_ref[g], both[:, cs], preferred_element_type=F32)
            count = jnp.minimum(pos + 1, w).astype(F32)
            diff = (wsum / count - cur[:, cs].astype(F32)).astype(BF16)
            y = jnp.dot(diff, wp_ref[g], preferred_element_type=F32) * sc_ref[:, cs]
            o_ref[rs, cs] = y.astype(o_ref.dtype)


def mixer_b(p, w_pool, scale, seq, chunks=4):
    t = p.shape[0]
    tm = chunks * BLOCK
    band = jnp.asarray(_pool_band_matrices(), BF16)
    col = COL_B // B_WIDTH
    return pl.pallas_call(
        functools.partial(_mixer_b_kernel, chunks=chunks, tiles_per_seq=seq // tm),
        grid=(t // tm,),
        in_specs=[pl.BlockSpec((BLOCK, B_WIDTH), lambda i: (jnp.maximum(i * chunks - 1, 0), col)),
                  pl.BlockSpec((tm, B_WIDTH), lambda i: (i, col)),
                  _resident((len(POOL_WINDOWS), BLOCK, 2 * BLOCK)),
                  _resident((len(POOL_WINDOWS), B_GROUP_DIM, B_GROUP_DIM)),
                  _resident((1, B_WIDTH))],
        out_specs=pl.BlockSpec((tm, B_WIDTH), lambda i: (i, 0)),
        out_shape=jax.ShapeDtypeStruct((t, B_WIDTH), BF16),
        compiler_params=_params(1),
        name="mixer_b",
    )(p, p, band, w_pool, scale.reshape(1, B_WIDTH))


def _rel_bucket(n):
    max_exact = REL_BUCKETS // 2
    n = np.asarray(n)
    nl = np.maximum(n, max_exact).astype(np.float64)
    large = max_exact + (np.log(nl / max_exact) / math.log(REL_MAX_DISTANCE / max_exact)
                         * (REL_BUCKETS - max_exact)).astype(np.int32)
    large = np.minimum(large, REL_BUCKETS - 1)
    return np.where(n < max_exact, n, large).astype(np.int32)


def _attn_bias(rel_table, g, dilation, n_off):
    qi = np.arange(BLOCK)[:, None]
    ki = np.arange(2 * BLOCK)[None, :]
    off = BLOCK + qi - ki
    ok = (off >= 0) & (off <= n_off)
    buckets = _rel_bucket(np.clip(off, 0, n_off) * dilation)
    heads = rel_table[:, g * C_HEADS_PER_GROUP:(g + 1) * C_HEADS_PER_GROUP]
    bias = jnp.transpose(heads[buckets], (2, 0, 1)).astype(F32)
    return jnp.where(ok[None], bias, NEG_MASK)


def _attn_kernel(q_ref, kp_ref, kc_ref, vp_ref, vc_ref, bias_ref, o_ref, lse_ref):
    first = pl.program_id(2) == 0
    key_is_prev = lax.broadcasted_iota(jnp.int32, (BLOCK, 2 * BLOCK), 1) < BLOCK
    hide = jnp.logical_and(first, key_is_prev)
    scale = C_HEAD_DIM ** -0.5
    for hd in range(C_HEADS_PER_GROUP):
        cs = slice(hd * C_HEAD_DIM, (hd + 1) * C_HEAD_DIM)
        keys = jnp.concatenate([kp_ref[:, cs], kc_ref[:, cs]], axis=0)
        vals = jnp.concatenate([vp_ref[:, cs], vc_ref[:, cs]], axis=0)
        logits = lax.dot_general(q_ref[:, cs], keys, (((1,), (1,)), ((), ())),
                                 preferred_element_type=F32) * scale
        logits = jnp.where(hide, NEG_MASK, logits + bias_ref[hd])
        m = jnp.max(logits, axis=-1, keepdims=True)
        e = jnp.exp(logits - m)
        s = jnp.sum(e, axis=-1, keepdims=True)
        o = jnp.dot(e.astype(BF16), vals, preferred_element_type=F32) / s
        o_ref[:, cs] = o.astype(o_ref.dtype)
        lse_ref[:, cs] = jnp.broadcast_to(m + jnp.log(s), (BLOCK, C_HEAD_DIM))


def dilated_attention_group(p, rel_table, g, batch, seq):
    window, dilation = C_PATTERNS[g]
    n_off = window // dilation
    sub_len = seq // dilation
    nb = sub_len // BLOCK
    width = p.shape[1]
    blocks_per_row = width // C_OUT_WIDTH
    pv = p.reshape(batch, sub_len, dilation * width)
    bias = _attn_bias(rel_table, g, dilation, n_off)

    def col(base):
        first = base // C_OUT_WIDTH + g
        return lambda b, r, n: (b, n, r * blocks_per_row + first)

    def col_prev(base):
        first = base // C_OUT_WIDTH + g
        return lambda b, r, n: (b, jnp.maximum(n - 1, 0), r * blocks_per_row + first)

    blk = (None, BLOCK, C_OUT_WIDTH)
    o, lse = pl.pallas_call(
        _attn_kernel,
        grid=(batch, dilation, nb),
        in_specs=[pl.BlockSpec(blk, col(COL_Q)),
                  pl.BlockSpec(blk, col_prev(COL_K)), pl.BlockSpec(blk, col(COL_K)),
                  pl.BlockSpec(blk, col_prev(COL_V)), pl.BlockSpec(blk, col(COL_V)),
                  _resident((C_HEADS_PER_GROUP, BLOCK, 2 * BLOCK))],
        out_specs=[pl.BlockSpec(blk, lambda b, r, n: (b, n, r)),
                   pl.BlockSpec(blk, lambda b, r, n: (b, n, r))],
        out_shape=[jax.ShapeDtypeStruct((batch, sub_len, dilation * C_OUT_WIDTH), BF16),
                   jax.ShapeDtypeStruct((batch, sub_len, dilation * C_OUT_WIDTH), F32)],
        compiler_params=_params(3),
        name=f"dilated_attention_{g}",
    )(pv, pv, pv, pv, pv, bias)
    t = batch * seq
    return o.reshape(t, C_OUT_WIDTH), lse.reshape(t, C_OUT_WIDTH)


def _merge_kernel(x_ref, gate_ref, ya_ref, yb_ref, o0_ref, o1_ref, o2_ref, l0_ref, l1_ref, l2_ref,
                  wa_ref, wb_ref, wc_ref, wo_ref, g_ref, xo_ref, h_ref):
    l0, l1, l2 = l0_ref[...], l1_ref[...], l2_ref[...]
    m = jnp.maximum(jnp.maximum(l0, l1), l2)
    e0, e1, e2 = jnp.exp(l0 - m), jnp.exp(l1 - m), jnp.exp(l2 - m)
    yc = (e0 * o0_ref[...].astype(F32) + e1 * o1_ref[...].astype(F32)
          + e2 * o2_ref[...].astype(F32)) / (e0 + e1 + e2)

    def branch(k, y, w_ref):
        gate = jax.nn.sigmoid(gate_ref[:, k * D_MODEL:(k + 1) * D_MODEL].astype(F32))
        return gate * jnp.dot(y, w_ref[...], preferred_element_type=F32)

    merged = (branch(0, ya_ref[...], wa_ref) + branch(1, yb_ref[...], wb_ref)
              + branch(2, yc.astype(BF16), wc_ref))
    x = x_ref[...] + jnp.dot(merged.astype(BF16), wo_ref[...], preferred_element_type=F32)
    xo_ref[...] = x
    h_ref[...] = _rms(x, g_ref[...]).astype(h_ref.dtype)


def merge_and_project(x, p, ya, yb, outs, lses, wa, wb, wc, wo, g_ffn, tm=256):
    t, d = x.shape
    row = lambda w: pl.BlockSpec((tm, w), lambda i: (i, 0))
    return pl.pallas_call(
        _merge_kernel,
        grid=(t // tm,),
        in_specs=[row(d), row(GATE_WIDTH), row(A_WIDTH), row(B_WIDTH),
                  row(C_OUT_WIDTH), row(C_OUT_WIDTH), row(C_OUT_WIDTH),
                  row(C_OUT_WIDTH), row(C_OUT_WIDTH), row(C_OUT_WIDTH),
                  _resident((A_WIDTH, d)), _resident((B_WIDTH, d)), _resident((C_OUT_WIDTH, d)),
                  _resident((d, d)), _resident((1, d))],
        out_specs=[row(d), row(d)],
        out_shape=[jax.ShapeDtypeStruct((t, d), F32), jax.ShapeDtypeStruct((t, d), BF16)],
        compiler_params=_params(1),
        name="merge_and_project",
    )(x, p, ya, yb, *outs, *lses, wa, wb, wc, wo, g_ffn.reshape(1, d))


def _candidate_pairs():
    return [(a, b) for a in range(PEER_TOPK) for b in range(PEER_TOPK // (a + 1))]


CAND_ROWS = 56


def _top16(s, want_rank):
    cur = s
    rank = jnp.full(s.shape, float(PEER_TOPK), F32)
    vals = []
    for k in range(PEER_TOPK):
        m = jnp.max(cur, axis=0, keepdims=True)
        hit = cur == m
        if want_rank:
            rank = jnp.where(hit, float(k), rank)
        cur = jnp.where(hit, -jnp.inf, cur)
        vals.append(m)
    return vals, rank


def _peer_select_kernel(h_ref, wq_ref, k1_ref, k2_ref, e2_ref, r2_ref, w_ref, n_ref, cand_ref):
    tm = h_ref.shape[0]
    qt = lax.dot_general(wq_ref[...], h_ref[...], (((1,), (1,)), ((), ())),
                         preferred_element_type=F32)
    half = PEER_KEY_DIM // 2
    cand_ref[...] = jnp.full((CAND_ROWS, tm), -jnp.inf, F32)
    for hd in range(PEER_HEADS):
        q1 = qt[hd * PEER_KEY_DIM:hd * PEER_KEY_DIM + half, :]
        q2 = qt[hd * PEER_KEY_DIM + half:(hd + 1) * PEER_KEY_DIM, :]
        s1 = jnp.dot(k1_ref[...], q1, preferred_element_type=F32, precision=lax.Precision.HIGHEST)
        s2 = jnp.dot(k2_ref[...], q2, preferred_element_type=F32, precision=lax.Precision.HIGHEST)
        v1, _ = _top16(s1, False)
        v2, rank2 = _top16(s2, True)
        for idx, (a, b) in enumerate(_candidate_pairs()):
            cand_ref[idx:idx + 1, :] = v1[a] + v2[b]
        best, _ = _top16(cand_ref[...], False)
        tau = best[PEER_TOPK - 1]
        z = functools.reduce(lambda acc, c: acc + jnp.exp(c - best[0]), best[1:],
                             jnp.ones_like(best[0]))
        count = jnp.zeros(s1.shape, F32)
        for b in range(PEER_TOPK):
            count = count + (s1 + v2[b] >= tau).astype(F32)
        e2_ref[hd] = jnp.exp(s2 - v2[0]).astype(e2_ref.dtype)
        r2_ref[hd] = rank2.astype(r2_ref.dtype)
        w_ref[hd] = jnp.exp(s1 - v1[0]) / z
        n_ref[hd] = count


def peer_select(h, wq_t, k1, k2, tm=256):
    t, d = h.shape
    tab = lambda dt: jax.ShapeDtypeStruct((PEER_HEADS, PEER_KEYS, t), dt)
    blk = pl.BlockSpec((PEER_HEADS, PEER_KEYS, tm), lambda i: (0, 0, i))
    return pl.pallas_call(
        _peer_select_kernel,
        grid=(t // tm,),
        in_specs=[pl.BlockSpec((tm, d), lambda i: (i, 0)),
                  _resident((PEER_HEADS * PEER_KEY_DIM, d)),
                  _resident((PEER_KEYS, PEER_KEY_DIM // 2)),
                  _resident((PEER_KEYS, PEER_KEY_DIM // 2))],
        out_specs=[blk, blk, blk, blk],
        out_shape=[tab(BF16), tab(BF16), tab(F32), tab(F32)],
        scratch_shapes=[pltpu.VMEM((CAND_ROWS, tm), F32)],
        compiler_params=_params(1),
        name="peer_select",
    )(h, wq_t, k1, k2)


def _peer_dense_kernel(h_ref, u_ref, vt_ref, e2_ref, r2_ref, w_ref, n_ref, x_ref, g_ref,
                       *rest, first_keys, final):
    if final:
        out_ref, acc_ref = rest
    else:
        xo_ref, hn_ref, acc_ref = rest
    e = pl.program_id(1)
    tm = h_ref.shape[0]

    @pl.when(e == 0)
    def _():
        acc_ref[...] = jnp.zeros_like(acc_ref)

    gates = []
    for ii in range(first_keys):
        g = jnp.zeros((PEER_KEYS, tm), BF16)
        for hd in range(PEER_HEADS):
            n_row = jnp.broadcast_to(n_ref[hd, ii:ii + 1, :].astype(BF16), (PEER_KEYS, tm))
            w_row = jnp.broadcast_to(w_ref[hd, ii:ii + 1, :].astype(BF16), (PEER_KEYS, tm))
            picked = jnp.where(r2_ref[hd] < n_row, e2_ref[hd], jnp.zeros((), BF16))
            g = g + picked * w_row
        gates.append(g)
    gate_t = jnp.concatenate(gates, axis=0)

    a_t = lax.dot_general(u_ref[...], h_ref[...], (((1,), (1,)), ((), ())),
                          preferred_element_type=F32)
    act_t = _gelu(a_t).astype(BF16) * gate_t
    acc_ref[...] += jnp.dot(vt_ref[...], act_t, preferred_element_type=F32)

    @pl.when(e == pl.num_programs(1) - 1)
    def _():
        x = x_ref[...] + acc_ref[...].T
        if final:
            out_ref[...] = _rms(x, g_ref[...])
        else:
            xo_ref[...] = x
            hn_ref[...] = _rms(x, g_ref[...]).astype(hn_ref.dtype)


def peer_dense(h, u, v_t, e2, r2, w, n, x, g_next, final, tm=512, te=512):
    t, d = h.shape
    first_keys = te // PEER_KEYS
    groups = PEER_KEYS // first_keys
    w4 = w.reshape(PEER_HEADS, groups, first_keys, t)
    n4 = n.reshape(PEER_HEADS, groups, first_keys, t)
    tab = pl.BlockSpec((PEER_HEADS, PEER_KEYS, tm), lambda i, e: (0, 0, i))
    rows = pl.BlockSpec((PEER_HEADS, None, first_keys, tm), lambda i, e: (0, e, 0, i))
    tok = pl.BlockSpec((tm, d), lambda i, e: (i, 0))
    if final:
        out_specs = tok
        out_shape = jax.ShapeDtypeStruct((t, d), F32)
    else:
        out_specs = [tok, tok]
        out_shape = [jax.ShapeDtypeStruct((t, d), F32), jax.ShapeDtypeStruct((t, d), BF16)]
    return pl.pallas_call(
        functools.partial(_peer_dense_kernel, first_keys=first_keys, final=final),
        grid=(t // tm, PEER_N_EXPERTS // te),
        in_specs=[tok,
                  pl.BlockSpec((te, d), lambda i, e: (e, 0)),
                  pl.BlockSpec((d, te), lambda i, e: (0, e)),
                  tab, tab, rows, rows, tok, _resident((1, d))],
        out_specs=out_specs,
        out_shape=out_shape,
        scratch_shapes=[pltpu.VMEM((d, tm), F32)],
        compiler_params=_params(2),
        name="peer_dense_final" if final else "peer_dense",
    )(h, u, v_t, e2, r2, w4, n4, x, g_next.reshape(1, d))


def kernel(x, rel_bias_table, norm_mix_g, w_in, a_norm_g, a_w_s, a_b_s, b_w_pool, b_scale, w_br_a,
           w_br_b, w_br_c, w_out, norm_ffn_g, peer_w_q, peer_sub_keys_1, peer_sub_keys_2, peer_u,
           peer_v, final_norm_g):
    batch, seq, d = x.shape
    t = batch * seq
    old_gate = IN_WIDTH - GATE_WIDTH
    xf = x.reshape(t, d)
    h = rmsnorm_rows(xf, norm_mix_g[0])
    out = None
    for l in range(DEPTH):
        w_in_l = jnp.concatenate([w_in[l][:, old_gate:], w_in[l][:, :old_gate]], axis=1).astype(BF16)
        p = in_proj(h, w_in_l)
        ya = mixer_a(p, a_norm_g[l], a_w_s[l], a_b_s[l])
        yb = mixer_b(p, b_w_pool[l].astype(BF16), b_scale[l], seq)
        outs, lses = [], []
        for g in range(C_GROUPS):
            o, lse = dilated_attention_group(p, rel_bias_table, g, batch, seq)
            outs.append(o)
            lses.append(lse)
        x_mid, h_ffn = merge_and_project(
            xf, p, ya, yb, outs, lses, w_br_a[l].astype(BF16), w_br_b[l].astype(BF16),
            w_br_c[l].astype(BF16), w_out[l].astype(BF16), norm_ffn_g[l])
        e2, r2, w, n = peer_select(h_ffn, peer_w_q[l].T.astype(BF16), peer_sub_keys_1[l],
                                   peer_sub_keys_2[l])
        final = l == DEPTH - 1
        g_next = final_norm_g if final else norm_mix_g[l + 1]
        res = peer_dense(h_ffn, peer_u[l].astype(BF16), peer_v[l].T.astype(BF16), e2, r2, w, n,
                         x_mid, g_next, final)
        if final:
            out = res
        else:
            xf, h = res
    return out.reshape(batch, seq, d)
```

```python
import functools
import math

import numpy as np
import jax
import jax.numpy as jnp
from jax import lax
from jax.experimental import pallas as pl
from jax.experimental.pallas import tpu as pltpu

D_MODEL = 2048
DEPTH = 2
BLOCK = 128
EPS = 1e-6
A_HEADS = 8
A_HEAD_DIM = 128
A_WIDTH = A_HEADS * A_HEAD_DIM
POOL_WINDOWS = (2, 4, 8, 16)
B_GROUP_DIM = 256
B_WIDTH = len(POOL_WINDOWS) * B_GROUP_DIM
C_PATTERNS = ((128, 1), (512, 4), (2048, 16))
C_GROUPS = len(C_PATTERNS)
C_HEADS_PER_GROUP = 4
C_HEAD_DIM = 128
C_HEADS = C_GROUPS * C_HEADS_PER_GROUP
C_QKV_WIDTH = C_HEADS * C_HEAD_DIM
C_OUT_WIDTH = C_HEADS_PER_GROUP * C_HEAD_DIM
N_BRANCHES = 3
GATE_WIDTH = N_BRANCHES * D_MODEL
IN_WIDTH = 2 * A_WIDTH + B_WIDTH + 3 * C_QKV_WIDTH + GATE_WIDTH
REL_BUCKETS = 32
REL_MAX_DISTANCE = 2048
PEER_HEADS = 8
PEER_KEYS = 128
PEER_N_EXPERTS = PEER_KEYS * PEER_KEYS
PEER_KEY_DIM = 256
PEER_TOPK = 16

COL_GATE = 0
COL_AU = GATE_WIDTH
COL_AV = COL_AU + A_WIDTH
COL_B = COL_AV + A_WIDTH
COL_QKV0 = COL_B + B_WIDTH
QKV_GROUP_WIDTH = 3 * C_OUT_WIDTH
MAIN_WIDTH = COL_QKV0 + QKV_GROUP_WIDTH

NEG_MASK = -1e30
BF16 = jnp.bfloat16
F32 = jnp.float32

VMEM_LIMIT_BYTES = 56 * 1024 * 1024


def _params(n_axes):
    return pltpu.CompilerParams(
        dimension_semantics=("arbitrary",) * n_axes, vmem_limit_bytes=VMEM_LIMIT_BYTES)


def _rms(x, g):
    return x * lax.rsqrt(jnp.mean(x * x, axis=-1, keepdims=True) + EPS) * g


def _gelu(x):
    return 0.5 * x * (1.0 + lax.erf(x * (1.0 / math.sqrt(2.0))))


def _resident(shape):
    nd = len(shape)
    return pl.BlockSpec(shape, lambda *_: (0,) * nd, pipeline_mode=pl.Buffered(1))


DILATIONS = tuple(d for _, d in C_PATTERNS if d > 1)


def _group_by_residue_matrices(tm):
    mats = []
    for dil in DILATIONS:
        dst = np.arange(tm)
        src = (dst % (tm // dil)) * dil + dst // (tm // dil)
        mats.append((src[:, None] == np.arange(tm)[None, :]).astype(np.float32))
    return np.stack(mats)


def _write_norms(x, g, perm_ref, h_ref, hd_refs):
    h = _rms(x, g).astype(h_ref.dtype)
    h_ref[...] = h
    for k, ref in enumerate(hd_refs):
        grouped = jnp.dot(perm_ref[k], h, preferred_element_type=F32).astype(ref.dtype)
        ref[...] = grouped.reshape(ref.shape)


def _norm_specs(batch, seq, d, tm, index):
    specs = [pl.BlockSpec((tm, d), lambda *ids: (index(*ids)[0] * (seq // tm) + index(*ids)[1], 0))]
    shapes = [jax.ShapeDtypeStruct((batch * seq, d), BF16)]
    for dil in DILATIONS:
        specs.append(pl.BlockSpec((None, dil, tm // dil, d),
                                  lambda *ids: (index(*ids)[0], 0, index(*ids)[1], 0)))
        shapes.append(jax.ShapeDtypeStruct((batch, dil, seq // dil, d), BF16))
    return specs, shapes


def _norm_kernel(x_ref, g_ref, perm_ref, h_ref, *hd_refs):
    _write_norms(x_ref[...], g_ref[...], perm_ref, h_ref, hd_refs)


def rmsnorm_rows(x, g, batch, seq, tm=512):
    t, d = x.shape
    tiles = seq // tm
    specs, shapes = _norm_specs(batch, seq, d, tm, lambda i: (i // tiles, i % tiles))
    perm = jnp.asarray(_group_by_residue_matrices(tm), BF16)
    return pl.pallas_call(
        _norm_kernel,
        grid=(t // tm,),
        in_specs=[pl.BlockSpec((tm, d), lambda i: (i, 0)), _resident((1, d)), _resident(perm.shape)],
        out_specs=specs,
        out_shape=shapes,
        compiler_params=_params(1),
        name="rmsnorm_rows",
    )(x, g.reshape(1, d), perm)


def _matmul_kernel(h_ref, w_ref, o_ref):
    o_ref[...] = jnp.dot(h_ref[...], w_ref[...], preferred_element_type=F32).astype(o_ref.dtype)


def in_proj(h, w, tm=1024, tn=1536):
    t, d = h.shape
    n = w.shape[1]
    return pl.pallas_call(
        _matmul_kernel,
        grid=(n // tn, t // tm),
        in_specs=[pl.BlockSpec((tm, d), lambda j, i: (i, 0)),
                  pl.BlockSpec((d, tn), lambda j, i: (0, j))],
        out_specs=pl.BlockSpec((tm, tn), lambda j, i: (i, j)),
        out_shape=jax.ShapeDtypeStruct((t, n), BF16),
        compiler_params=_params(2),
        name="in_proj",
    )(h, w)


def _mixer_a_kernel(au_ref, av_ref, g_ref, ws_ref, bs_ref, o_ref, *, chunks):
    u = _gelu(au_ref[...].astype(F32))
    v = _rms(_gelu(av_ref[...].astype(F32)), g_ref[...]).astype(BF16)
    row = lax.broadcasted_iota(jnp.int32, (BLOCK, BLOCK), 0)
    col = lax.broadcasted_iota(jnp.int32, (BLOCK, BLOCK), 1)
    causal = col <= row
    for hd in range(A_HEADS):
        w = jnp.where(causal, ws_ref[hd], 0.0).astype(BF16)
        bias = bs_ref[hd]
        cs = slice(hd * A_HEAD_DIM, (hd + 1) * A_HEAD_DIM)
        for c in range(chunks):
            rs = slice(c * BLOCK, (c + 1) * BLOCK)
            s = jnp.dot(w, v[rs, cs], preferred_element_type=F32) + bias
            o_ref[rs, cs] = (u[rs, cs] * s).astype(o_ref.dtype)


def mixer_a(p, a_norm_g, a_w_s, a_b_s, chunks=4):
    t = p.shape[0]
    tm = chunks * BLOCK
    bias = jnp.broadcast_to(a_b_s[:, :, None], (A_HEADS, BLOCK, A_HEAD_DIM))
    return pl.pallas_call(
        functools.partial(_mixer_a_kernel, chunks=chunks),
        grid=(t // tm,),
        in_specs=[pl.BlockSpec((tm, A_WIDTH), lambda i: (i, COL_AU // A_WIDTH)),
                  pl.BlockSpec((tm, A_WIDTH), lambda i: (i, COL_AV // A_WIDTH)),
                  _resident((1, A_WIDTH)),
                  _resident((A_HEADS, BLOCK, BLOCK)),
                  _resident((A_HEADS, BLOCK, A_HEAD_DIM))],
        out_specs=pl.BlockSpec((tm, A_WIDTH), lambda i: (i, 0)),
        out_shape=jax.ShapeDtypeStruct((t, A_WIDTH), BF16),
        compiler_params=_params(1),
        name="mixer_a",
    )(p, p, a_norm_g.reshape(1, A_WIDTH), a_w_s, bias)


def _pool_band_matrices():
    t = np.arange(BLOCK)[:, None]
    k = np.arange(2 * BLOCK)[None, :]
    back = BLOCK + t - k
    return np.stack([((back >= 0) & (back < w)) for w in POOL_WINDOWS]).astype(np.float32)


def _mixer_b_kernel(prev_ref, cur_ref, band_ref, wp_ref, sc_ref, o_ref, *, chunks, tiles_per_seq):
    first = (pl.program_id(0) % tiles_per_seq) == 0
    prev = jnp.where(first, jnp.zeros_like(prev_ref[...]), prev_ref[...])
    tile_pos = (pl.program_id(0) % tiles_per_seq) * (chunks * BLOCK)
    for c in range(chunks):
        rs = slice(c * BLOCK, (c + 1) * BLOCK)
        before = prev if c == 0 else cur_ref[(c - 1) * BLOCK:c * BLOCK, :]
        cur = cur_ref[rs, :]
        both = jnp.concatenate([before, cur], axis=0)
        pos = tile_pos + c * BLOCK + lax.broadcasted_iota(jnp.int32, (BLOCK, B_GROUP_DIM), 0)
        for g, w in enumerate(POOL_WINDOWS):
            cs = slice(g * B_GROUP_DIM, (g + 1) * B_GROUP_DIM)
            wsum = jnp.dot(band_ref[g], both[:, cs], preferred_element_type=F32)
            count = jnp.minimum(pos + 1, w).astype(F32)
            diff = (wsum / count - cur[:, cs].astype(F32)).astype(BF16)
            y = jnp.dot(diff, wp_ref[g], preferred_element_type=F32) * sc_ref[:, cs]
            o_ref[rs, cs] = y.astype(o_ref.dtype)


def mixer_b(p, w_pool, scale, seq, chunks=4):
    t = p.shape[0]
    tm = chunks * BLOCK
    band = jnp.asarray(_pool_band_matrices(), BF16)
    col = COL_B // B_WIDTH
    return pl.pallas_call(
        functools.partial(_mixer_b_kernel, chunks=chunks, tiles_per_seq=seq // tm),
        grid=(t // tm,),
        in_specs=[pl.BlockSpec((BLOCK, B_WIDTH), lambda i: (jnp.maximum(i * chunks - 1, 0), col)),
                  pl.BlockSpec((tm, B_WIDTH), lambda i: (i, col)),
                  _resident((len(POOL_WINDOWS), BLOCK, 2 * BLOCK)),
                  _resident((len(POOL_WINDOWS), B_GROUP_DIM, B_GROUP_DIM)),
                  _resident((1, B_WIDTH))],
        out_specs=pl.BlockSpec((tm, B_WIDTH), lambda i: (i, 0)),
        out_shape=jax.ShapeDtypeStruct((t, B_WIDTH), BF16),
        compiler_params=_params(1),
        name="mixer_b",
    )(p, p, band, w_pool, scale.reshape(1, B_WIDTH))


def _rel_bucket(n):
    max_exact = REL_BUCKETS // 2
    n = np.asarray(n)
    nl = np.maximum(n, max_exact).astype(np.float64)
    large = max_exact + (np.log(nl / max_exact) / math.log(REL_MAX_DISTANCE / max_exact)
                         * (REL_BUCKETS - max_exact)).astype(np.int32)
    large = np.minimum(large, REL_BUCKETS - 1)
    return np.where(n < max_exact, n, large).astype(np.int32)


def _attn_bias(rel_table, g, dilation, n_off):
    qi = np.arange(BLOCK)[:, None]
    ki = np.arange(2 * BLOCK)[None, :]
    off = BLOCK + qi - ki
    ok = (off >= 0) & (off <= n_off)
    buckets = _rel_bucket(np.clip(off, 0, n_off) * dilation)
    onehot = (buckets.reshape(1, -1) == np.arange(REL_BUCKETS)[:, None]).astype(np.float32)
    heads = rel_table[:, g * C_HEADS_PER_GROUP:(g + 1) * C_HEADS_PER_GROUP]
    bias = jnp.dot(heads.T.astype(F32), jnp.asarray(onehot), precision=lax.Precision.HIGHEST)
    bias = bias.reshape(C_HEADS_PER_GROUP, BLOCK, 2 * BLOCK)
    return jnp.where(ok[None], bias, NEG_MASK)


def _attn_kernel(q_ref, kp_ref, kc_ref, vp_ref, vc_ref, bias_ref, o_ref, lse_ref):
    first = pl.program_id(1) == 0
    key_is_prev = lax.broadcasted_iota(jnp.int32, (BLOCK, 2 * BLOCK), 1) < BLOCK
    hide = jnp.logical_and(first, key_is_prev)
    scale = C_HEAD_DIM ** -0.5
    for hd in range(C_HEADS_PER_GROUP):
        cs = slice(hd * C_HEAD_DIM, (hd + 1) * C_HEAD_DIM)
        keys = jnp.concatenate([kp_ref[:, cs], kc_ref[:, cs]], axis=0)
        vals = jnp.concatenate([vp_ref[:, cs], vc_ref[:, cs]], axis=0)
        logits = lax.dot_general(q_ref[:, cs], keys, (((1,), (1,)), ((), ())),
                                 preferred_element_type=F32) * scale
        logits = jnp.where(hide, NEG_MASK, logits + bias_ref[hd])
        m = jnp.max(logits, axis=-1, keepdims=True)
        e = jnp.exp(logits - m)
        s = jnp.sum(e, axis=-1, keepdims=True)
        o = jnp.dot(e.astype(BF16), vals, preferred_element_type=F32) / s
        o_ref[:, cs] = o.astype(o_ref.dtype)
        lse_ref[:, cs] = jnp.broadcast_to(m + jnp.log(s), (BLOCK, C_HEAD_DIM))


def dilated_attention_group(qkv, first_col, rel_table, g, batch, seq):
    window, dilation = C_PATTERNS[g]
    n_off = window // dilation
    sub_len = seq // dilation
    nb = sub_len // BLOCK
    subs = batch * dilation
    pv = qkv.reshape(subs, sub_len, qkv.shape[1])
    bias = _attn_bias(rel_table, g, dilation, n_off)
    c0 = first_col // C_OUT_WIDTH

    def col(k):
        return lambda s, n: (s, n, c0 + k)

    def col_prev(k):
        return lambda s, n: (s, jnp.maximum(n - 1, 0), c0 + k)

    blk = (None, BLOCK, C_OUT_WIDTH)
    out_blk = pl.BlockSpec(blk, lambda s, n: (s, n, 0))
    o, lse = pl.pallas_call(
        _attn_kernel,
        grid=(subs, nb),
        in_specs=[pl.BlockSpec(blk, col(0)),
                  pl.BlockSpec(blk, col_prev(1)), pl.BlockSpec(blk, col(1)),
                  pl.BlockSpec(blk, col_prev(2)), pl.BlockSpec(blk, col(2)),
                  _resident((C_HEADS_PER_GROUP, BLOCK, 2 * BLOCK))],
        out_specs=[out_blk, out_blk],
        out_shape=[jax.ShapeDtypeStruct((subs, sub_len, C_OUT_WIDTH), BF16),
                   jax.ShapeDtypeStruct((subs, sub_len, C_OUT_WIDTH), F32)],
        compiler_params=_params(2),
        name=f"dilated_attention_{g}",
    )(pv, pv, pv, pv, pv, bias)
    return o, lse


def _merge_kernel(x_ref, gate_ref, ya_ref, yb_ref, o0_ref, o1_ref, o2_ref, l0_ref, l1_ref, l2_ref,
                  wa_ref, wb_ref, wc_ref, wo_ref, g_ref, perm_ref, xo_ref, h_ref):
    tm = x_ref.shape[0]

    def token_order(ref, k, precision=None):
        return jnp.dot(perm_ref[k].astype(ref.dtype), ref[...].reshape(tm, C_OUT_WIDTH),
                       preferred_element_type=F32, precision=precision)

    exact_f32 = lax.Precision.HIGHEST
    l0, l1, l2 = l0_ref[...], token_order(l1_ref, 0, exact_f32), token_order(l2_ref, 1, exact_f32)
    m = jnp.maximum(jnp.maximum(l0, l1), l2)
    e0, e1, e2 = jnp.exp(l0 - m), jnp.exp(l1 - m), jnp.exp(l2 - m)
    yc = (e0 * o0_ref[...].astype(F32) + e1 * token_order(o1_ref, 0)
          + e2 * token_order(o2_ref, 1)) / (e0 + e1 + e2)

    def branch(k, y, w_ref):
        gate = jax.nn.sigmoid(gate_ref[:, k * D_MODEL:(k + 1) * D_MODEL].astype(F32))
        return gate * jnp.dot(y, w_ref[...], preferred_element_type=F32)

    merged = (branch(0, ya_ref[...], wa_ref) + branch(1, yb_ref[...], wb_ref)
              + branch(2, yc.astype(BF16), wc_ref))
    x = x_ref[...] + jnp.dot(merged.astype(BF16), wo_ref[...], preferred_element_type=F32)
    xo_ref[...] = x
    h_ref[...] = _rms(x, g_ref[...]).astype(h_ref.dtype)


def merge_and_project(x, p, ya, yb, outs, lses, wa, wb, wc, wo, g_ffn, batch, seq, tm=256):
    t, d = x.shape
    tiles = seq // tm
    row = lambda w: pl.BlockSpec((tm, w), lambda i: (i, 0))

    def grouped(dil):
        return pl.BlockSpec((None, dil, tm // dil, C_OUT_WIDTH), lambda i: (i // tiles, 0, i % tiles, 0))

    def by_residue(a, dil):
        return a.reshape(batch, dil, seq // dil, C_OUT_WIDTH)

    d1, d2 = DILATIONS
    perm = jnp.asarray(_group_by_residue_matrices(tm).transpose(0, 2, 1), BF16)
    return pl.pallas_call(
        _merge_kernel,
        grid=(t // tm,),
        in_specs=[row(d), row(GATE_WIDTH), row(A_WIDTH), row(B_WIDTH),
                  row(C_OUT_WIDTH), grouped(d1), grouped(d2),
                  row(C_OUT_WIDTH), grouped(d1), grouped(d2),
                  _resident((A_WIDTH, d)), _resident((B_WIDTH, d)), _resident((C_OUT_WIDTH, d)),
                  _resident((d, d)), _resident((1, d)), _resident(perm.shape)],
        out_specs=[row(d), row(d)],
        out_shape=[jax.ShapeDtypeStruct((t, d), F32), jax.ShapeDtypeStruct((t, d), BF16)],
        compiler_params=_params(1),
        name="merge_and_project",
    )(x, p, ya, yb,
      outs[0].reshape(t, C_OUT_WIDTH), by_residue(outs[1], d1), by_residue(outs[2], d2),
      lses[0].reshape(t, C_OUT_WIDTH), by_residue(lses[1], d1), by_residue(lses[2], d2),
      wa, wb, wc, wo, g_ffn.reshape(1, d), perm)


def _candidate_pairs():
    return [(a, b) for a in range(PEER_TOPK) for b in range(PEER_TOPK // (a + 1))]


CAND_ROWS = 56


def _top16(s, want_rank):
    cur = s
    rank = jnp.full(s.shape, float(PEER_TOPK), F32)
    vals = []
    for k in range(PEER_TOPK):
        m = jnp.max(cur, axis=0, keepdims=True)
        hit = cur == m
        if want_rank:
            rank = jnp.where(hit, float(k), rank)
        cur = jnp.where(hit, -jnp.inf, cur)
        vals.append(m)
    return vals, rank


def _peer_select_kernel(h_ref, wq_ref, k1_ref, k2_ref, e2_ref, r2_ref, w_ref, n_ref, cand_ref):
    tm = h_ref.shape[0]
    qt = lax.dot_general(wq_ref[...], h_ref[...], (((1,), (1,)), ((), ())),
                         preferred_element_type=F32)
    half = PEER_KEY_DIM // 2
    cand_ref[...] = jnp.full((CAND_ROWS, tm), -jnp.inf, F32)
    for hd in range(PEER_HEADS):
        q1 = qt[hd * PEER_KEY_DIM:hd * PEER_KEY_DIM + half, :]
        q2 = qt[hd * PEER_KEY_DIM + half:(hd + 1) * PEER_KEY_DIM, :]
        s1 = jnp.dot(k1_ref[...], q1, preferred_element_type=F32, precision=lax.Precision.HIGHEST)
        s2 = jnp.dot(k2_ref[...], q2, preferred_element_type=F32, precision=lax.Precision.HIGHEST)
        v1, _ = _top16(s1, False)
        v2, rank2 = _top16(s2, True)
        for idx, (a, b) in enumerate(_candidate_pairs()):
            cand_ref[idx:idx + 1, :] = v1[a] + v2[b]
        best, _ = _top16(cand_ref[...], False)
        tau = best[PEER_TOPK - 1]
        z = functools.reduce(lambda acc, c: acc + jnp.exp(c - best[0]), best[1:],
                             jnp.ones_like(best[0]))
        count = jnp.zeros(s1.shape, F32)
        for b in range(PEER_TOPK):
            count = count + (s1 + v2[b] >= tau).astype(F32)
        e2_ref[hd] = jnp.exp(s2 - v2[0]).astype(e2_ref.dtype)
        r2_ref[hd] = rank2.astype(r2_ref.dtype)
        w_ref[hd] = jnp.exp(s1 - v1[0]) / z
        n_ref[hd] = count


def peer_select(h, wq_t, k1, k2, tm=256):
    t, d = h.shape
    tab = lambda dt: jax.ShapeDtypeStruct((PEER_HEADS, PEER_KEYS, t), dt)
    blk = pl.BlockSpec((PEER_HEADS, PEER_KEYS, tm), lambda i: (0, 0, i))
    return pl.pallas_call(
        _peer_select_kernel,
        grid=(t // tm,),
        in_specs=[pl.BlockSpec((tm, d), lambda i: (i, 0)),
                  _resident((PEER_HEADS * PEER_KEY_DIM, d)),
                  _resident((PEER_KEYS, PEER_KEY_DIM // 2)),
                  _resident((PEER_KEYS, PEER_KEY_DIM // 2))],
        out_specs=[blk, blk, blk, blk],
        out_shape=[tab(BF16), tab(BF16), tab(F32), tab(F32)],
        scratch_shapes=[pltpu.VMEM((CAND_ROWS, tm), F32)],
        compiler_params=_params(1),
        name="peer_select",
    )(h, wq_t, k1, k2)


def _peer_dense_kernel(h_ref, u_ref, vt_ref, e2_ref, r2_ref, w_ref, n_ref, x_ref, g_ref,
                       *rest, first_keys, final):
    if final:
        out_ref, acc_ref = rest
    else:
        perm_ref, xo_ref, hn_ref, *hd_refs, acc_ref = rest
    e = pl.program_id(1)
    tm = h_ref.shape[0]

    @pl.when(e == 0)
    def _():
        acc_ref[...] = jnp.zeros_like(acc_ref)

    gates = []
    for ii in range(first_keys):
        g = jnp.zeros((PEER_KEYS, tm), BF16)
        for hd in range(PEER_HEADS):
            n_row = jnp.broadcast_to(n_ref[hd, ii:ii + 1, :].astype(BF16), (PEER_KEYS, tm))
            w_row = jnp.broadcast_to(w_ref[hd, ii:ii + 1, :].astype(BF16), (PEER_KEYS, tm))
            picked = jnp.where(r2_ref[hd] < n_row, e2_ref[hd], jnp.zeros((), BF16))
            g = g + picked * w_row
        gates.append(g)
    gate_t = jnp.concatenate(gates, axis=0)

    a_t = lax.dot_general(u_ref[...], h_ref[...], (((1,), (1,)), ((), ())),
                          preferred_element_type=F32)
    act_t = _gelu(a_t).astype(BF16) * gate_t
    acc_ref[...] += jnp.dot(vt_ref[...], act_t, preferred_element_type=F32)

    @pl.when(e == pl.num_programs(1) - 1)
    def _():
        x = x_ref[...] + acc_ref[...].T
        if final:
            out_ref[...] = _rms(x, g_ref[...])
        else:
            xo_ref[...] = x
            _write_norms(x, g_ref[...], perm_ref, hn_ref, hd_refs)


def peer_dense(h, u, v_t, e2, r2, w, n, x, g_next, final, batch, seq, tm=512, te=512):
    t, d = h.shape
    tiles = seq // tm
    first_keys = te // PEER_KEYS
    groups = PEER_KEYS // first_keys
    w4 = w.reshape(PEER_HEADS, groups, first_keys, t)
    n4 = n.reshape(PEER_HEADS, groups, first_keys, t)
    tab = pl.BlockSpec((PEER_HEADS, PEER_KEYS, tm), lambda i, e: (0, 0, i))
    rows = pl.BlockSpec((PEER_HEADS, None, first_keys, tm), lambda i, e: (0, e, 0, i))
    tok = pl.BlockSpec((tm, d), lambda i, e: (i, 0))
    operands = [h, u, v_t, e2, r2, w4, n4, x, g_next.reshape(1, d)]
    in_specs = [tok,
                pl.BlockSpec((te, d), lambda i, e: (e, 0)),
                pl.BlockSpec((d, te), lambda i, e: (0, e)),
                tab, tab, rows, rows, tok, _resident((1, d))]
    if final:
        out_specs = tok
        out_shape = jax.ShapeDtypeStruct((t, d), F32)
    else:
        specs, shapes = _norm_specs(batch, seq, d, tm, lambda i, e: (i // tiles, i % tiles))
        out_specs = [tok] + specs
        out_shape = [jax.ShapeDtypeStruct((t, d), F32)] + shapes
        perm = jnp.asarray(_group_by_residue_matrices(tm), BF16)
        operands.append(perm)
        in_specs.append(_resident(perm.shape))
    return pl.pallas_call(
        functools.partial(_peer_dense_kernel, first_keys=first_keys, final=final),
        grid=(t // tm, PEER_N_EXPERTS // te),
        in_specs=in_specs,
        out_specs=out_specs,
        out_shape=out_shape,
        scratch_shapes=[pltpu.VMEM((d, tm), F32)],
        compiler_params=_params(2),
        name="peer_dense_final" if final else "peer_dense",
    )(*operands)


def _split_w_in(w):
    cuts = np.cumsum([0, A_WIDTH, A_WIDTH, B_WIDTH, C_QKV_WIDTH, C_QKV_WIDTH, C_QKV_WIDTH, GATE_WIDTH])
    au, av, b, q, k, v, gates = (w[:, lo:hi] for lo, hi in zip(cuts[:-1], cuts[1:]))
    group = lambda g: [m[:, g * C_OUT_WIDTH:(g + 1) * C_OUT_WIDTH] for m in (q, k, v)]
    main = jnp.concatenate([gates, au, av, b] + group(0), axis=1).astype(BF16)
    return main, [jnp.concatenate(group(g), axis=1).astype(BF16) for g in range(1, C_GROUPS)]


def kernel(x, rel_bias_table, norm_mix_g, w_in, a_norm_g, a_w_s, a_b_s, b_w_pool, b_scale, w_br_a,
           w_br_b, w_br_c, w_out, norm_ffn_g, peer_w_q, peer_sub_keys_1, peer_sub_keys_2, peer_u,
           peer_v, final_norm_g):
    batch, seq, d = x.shape
    t = batch * seq
    xf = x.reshape(t, d)
    hs = rmsnorm_rows(xf, norm_mix_g[0], batch, seq)
    out = None
    for l in range(DEPTH):
        w_main, w_groups = _split_w_in(w_in[l])
        p = in_proj(hs[0], w_main)
        ya = mixer_a(p, a_norm_g[l], a_w_s[l], a_b_s[l])
        yb = mixer_b(p, b_w_pool[l].astype(BF16), b_scale[l], seq)
        outs, lses = [], []
        for g in range(C_GROUPS):
            if g == 0:
                qkv, first_col = p, COL_QKV0
            else:
                qkv, first_col = in_proj(hs[g].reshape(t, d), w_groups[g - 1], tn=QKV_GROUP_WIDTH), 0
            o, lse = dilated_attention_group(qkv, first_col, rel_bias_table, g, batch, seq)
            outs.append(o)
            lses.append(lse)
        x_mid, h_ffn = merge_and_project(
            xf, p, ya, yb, outs, lses, w_br_a[l].astype(BF16), w_br_b[l].astype(BF16),
            w_br_c[l].astype(BF16), w_out[l].astype(BF16), norm_ffn_g[l], batch, seq)
        e2, r2, w, n = peer_select(h_ffn, peer_w_q[l].T.astype(BF16), peer_sub_keys_1[l],
                                   peer_sub_keys_2[l])
        final = l == DEPTH - 1
        g_next = final_norm_g if final else norm_mix_g[l + 1]
        res = peer_dense(h_ffn, peer_u[l].astype(BF16), peer_v[l].T.astype(BF16), e2, r2, w, n,
                         x_mid, g_next, final, batch, seq)
        if final:
            out = res
        else:
            xf, hs = res[0], res[1:]
    return out.reshape(batch, seq, d)
```

```python
import functools
import math

import numpy as np
import jax
import jax.numpy as jnp
from jax import lax
from jax.experimental import pallas as pl
from jax.experimental.pallas import tpu as pltpu

D_MODEL = 2048
DEPTH = 2
BLOCK = 128
EPS = 1e-6
A_HEADS = 8
A_HEAD_DIM = 128
A_WIDTH = A_HEADS * A_HEAD_DIM
POOL_WINDOWS = (2, 4, 8, 16)
B_GROUP_DIM = 256
B_WIDTH = len(POOL_WINDOWS) * B_GROUP_DIM
C_PATTERNS = ((128, 1), (512, 4), (2048, 16))
C_GROUPS = len(C_PATTERNS)
C_HEADS_PER_GROUP = 4
C_HEAD_DIM = 128
C_HEADS = C_GROUPS * C_HEADS_PER_GROUP
C_QKV_WIDTH = C_HEADS * C_HEAD_DIM
C_OUT_WIDTH = C_HEADS_PER_GROUP * C_HEAD_DIM
N_BRANCHES = 3
GATE_WIDTH = N_BRANCHES * D_MODEL
IN_WIDTH = 2 * A_WIDTH + B_WIDTH + 3 * C_QKV_WIDTH + GATE_WIDTH
REL_BUCKETS = 32
REL_MAX_DISTANCE = 2048
PEER_HEADS = 8
PEER_KEYS = 128
PEER_N_EXPERTS = PEER_KEYS * PEER_KEYS
PEER_KEY_DIM = 256
PEER_TOPK = 16

COL_GATE = 0
COL_AU = GATE_WIDTH
COL_AV = COL_AU + A_WIDTH
COL_B = COL_AV + A_WIDTH
COL_QKV0 = COL_B + B_WIDTH
QKV_GROUP_WIDTH = 3 * C_OUT_WIDTH
MAIN_WIDTH = COL_QKV0 + QKV_GROUP_WIDTH

NEG_MASK = -1e30
BF16 = jnp.bfloat16
F32 = jnp.float32

VMEM_LIMIT_BYTES = 56 * 1024 * 1024
LANES = 128
PACKED_ROWS = 16
MXU_COLS = 256


def _params(n_axes):
    return pltpu.CompilerParams(
        dimension_semantics=("arbitrary",) * n_axes, vmem_limit_bytes=VMEM_LIMIT_BYTES)


def _rms(x, g):
    return x * lax.rsqrt(jnp.mean(x * x, axis=-1, keepdims=True) + EPS) * g


def _gelu(x):
    return 0.5 * x * (1.0 + lax.erf(x * (1.0 / math.sqrt(2.0))))


def _resident(shape):
    nd = len(shape)
    return pl.BlockSpec(shape, lambda *_: (0,) * nd, pipeline_mode=pl.Buffered(1))


DILATIONS = tuple(d for _, d in C_PATTERNS if d > 1)


def _group_by_residue_matrices(tm):
    mats = []
    for dil in DILATIONS:
        dst = np.arange(tm)
        src = (dst % (tm // dil)) * dil + dst // (tm // dil)
        mats.append((src[:, None] == np.arange(tm)[None, :]).astype(np.float32))
    return np.stack(mats)


def _write_norms(x, g, perm_ref, h_ref, hd_refs):
    h = _rms(x, g).astype(h_ref.dtype)
    h_ref[...] = h
    for k, ref in enumerate(hd_refs):
        grouped = jnp.dot(perm_ref[k], h, preferred_element_type=F32).astype(ref.dtype)
        ref[...] = grouped.reshape(ref.shape)


def _norm_specs(batch, seq, d, tm, index):
    specs = [pl.BlockSpec((tm, d), lambda *ids: (index(*ids)[0] * (seq // tm) + index(*ids)[1], 0))]
    shapes = [jax.ShapeDtypeStruct((batch * seq, d), BF16)]
    for dil in DILATIONS:
        specs.append(pl.BlockSpec((None, dil, tm // dil, d),
                                  lambda *ids: (index(*ids)[0], 0, index(*ids)[1], 0)))
        shapes.append(jax.ShapeDtypeStruct((batch, dil, seq // dil, d), BF16))
    return specs, shapes


def _norm_kernel(x_ref, g_ref, perm_ref, h_ref, *hd_refs):
    _write_norms(x_ref[...], g_ref[...], perm_ref, h_ref, hd_refs)


def rmsnorm_rows(x, g, batch, seq, tm=512):
    t, d = x.shape
    tiles = seq // tm
    specs, shapes = _norm_specs(batch, seq, d, tm, lambda i: (i // tiles, i % tiles))
    perm = jnp.asarray(_group_by_residue_matrices(tm), BF16)
    return pl.pallas_call(
        _norm_kernel,
        grid=(t // tm,),
        in_specs=[pl.BlockSpec((tm, d), lambda i: (i, 0)), _resident((1, d)), _resident(perm.shape)],
        out_specs=specs,
        out_shape=shapes,
        compiler_params=_params(1),
        name="rmsnorm_rows",
    )(x, g.reshape(1, d), perm)


def _matmul_kernel(h_ref, w_ref, o_ref):
    o_ref[...] = jnp.dot(h_ref[...], w_ref[...], preferred_element_type=F32).astype(o_ref.dtype)


def in_proj(h, w, tm=1024, tn=1536):
    t, d = h.shape
    n = w.shape[1]
    return pl.pallas_call(
        _matmul_kernel,
        grid=(n // tn, t // tm),
        in_specs=[pl.BlockSpec((tm, d), lambda j, i: (i, 0)),
                  pl.BlockSpec((d, tn), lambda j, i: (0, j))],
        out_specs=pl.BlockSpec((tm, tn), lambda j, i: (i, j)),
        out_shape=jax.ShapeDtypeStruct((t, n), BF16),
        compiler_params=_params(2),
        name="in_proj",
    )(h, w)


def _mixer_a_kernel(au_ref, av_ref, g_ref, ws_ref, bs_ref, o_ref, *, chunks):
    u = _gelu(au_ref[...].astype(F32))
    v = _rms(_gelu(av_ref[...].astype(F32)), g_ref[...]).astype(BF16)
    row = lax.broadcasted_iota(jnp.int32, (BLOCK, BLOCK), 0)
    col = lax.broadcasted_iota(jnp.int32, (BLOCK, BLOCK), 1)
    causal = col <= row
    for hd in range(A_HEADS):
        w = jnp.where(causal, ws_ref[hd], 0.0).astype(BF16)
        bias = bs_ref[hd]
        cs = slice(hd * A_HEAD_DIM, (hd + 1) * A_HEAD_DIM)
        for c in range(chunks):
            rs = slice(c * BLOCK, (c + 1) * BLOCK)
            s = jnp.dot(w, v[rs, cs], preferred_element_type=F32) + bias
            o_ref[rs, cs] = (u[rs, cs] * s).astype(o_ref.dtype)


def mixer_a(p, a_norm_g, a_w_s, a_b_s, chunks=4):
    t = p.shape[0]
    tm = chunks * BLOCK
    bias = jnp.broadcast_to(a_b_s[:, :, None], (A_HEADS, BLOCK, A_HEAD_DIM))
    return pl.pallas_call(
        functools.partial(_mixer_a_kernel, chunks=chunks),
        grid=(t // tm,),
        in_specs=[pl.BlockSpec((tm, A_WIDTH), lambda i: (i, COL_AU // A_WIDTH)),
                  pl.BlockSpec((tm, A_WIDTH), lambda i: (i, COL_AV // A_WIDTH)),
                  _resident((1, A_WIDTH)),
                  _resident((A_HEADS, BLOCK, BLOCK)),
                  _resident((A_HEADS, BLOCK, A_HEAD_DIM))],
        out_specs=pl.BlockSpec((tm, A_WIDTH), lambda i: (i, 0)),
        out_shape=jax.ShapeDtypeStruct((t, A_WIDTH), BF16),
        compiler_params=_params(1),
        name="mixer_a",
    )(p, p, a_norm_g.reshape(1, A_WIDTH), a_w_s, bias)


def _pool_band_matrices():
    t = np.arange(BLOCK)[:, None]
    k = np.arange(2 * BLOCK)[None, :]
    back = BLOCK + t - k
    return np.stack([((back >= 0) & (back < w)) for w in POOL_WINDOWS]).astype(np.float32)


def _mixer_b_kernel(prev_ref, cur_ref, band_ref, wp_ref, sc_ref, o_ref, *, chunks, tiles_per_seq):
    first = (pl.program_id(0) % tiles_per_seq) == 0
    prev = jnp.where(first, jnp.zeros_like(prev_ref[...]), prev_ref[...])
    tile_pos = (pl.program_id(0) % tiles_per_seq) * (chunks * BLOCK)
    for c in range(chunks):
        rs = slice(c * BLOCK, (c + 1) * BLOCK)
        before = prev if c == 0 else cur_ref[(c - 1) * BLOCK:c * BLOCK, :]
        cur = cur_ref[rs, :]
        both = jnp.concatenate([before, cur], axis=0)
        pos = tile_pos + c * BLOCK + lax.broadcasted_iota(jnp.int32, (BLOCK, B_GROUP_DIM), 0)
        for g, w in enumerate(POOL_WINDOWS):
            cs = slice(g * B_GROUP_DIM, (g + 1) * B_GROUP_DIM)
            wsum = jnp.dot(band_ref[g], both[:, cs], preferred_element_type=F32)
            count = jnp.minimum(pos + 1, w).astype(F32)
            diff = (wsum / count - cur[:, cs].astype(F32)).astype(BF16)
            y = jnp.dot(diff, wp_ref[g], preferred_element_type=F32) * sc_ref[:, cs]
            o_ref[rs, cs] = y.astype(o_ref.dtype)


def mixer_b(p, w_pool, scale, seq, chunks=4):
    t = p.shape[0]
    tm = chunks * BLOCK
    band = jnp.asarray(_pool_band_matrices(), BF16)
    col = COL_B // B_WIDTH
    return pl.pallas_call(
        functools.partial(_mixer_b_kernel, chunks=chunks, tiles_per_seq=seq // tm),
        grid=(t // tm,),
        in_specs=[pl.BlockSpec((BLOCK, B_WIDTH), lambda i: (jnp.maximum(i * chunks - 1, 0), col)),
                  pl.BlockSpec((tm, B_WIDTH), lambda i: (i, col)),
                  _resident((len(POOL_WINDOWS), BLOCK, 2 * BLOCK)),
                  _resident((len(POOL_WINDOWS), B_GROUP_DIM, B_GROUP_DIM)),
                  _resident((1, B_WIDTH))],
        out_specs=pl.BlockSpec((tm, B_WIDTH), lambda i: (i, 0)),
        out_shape=jax.ShapeDtypeStruct((t, B_WIDTH), BF16),
        compiler_params=_params(1),
        name="mixer_b",
    )(p, p, band, w_pool, scale.reshape(1, B_WIDTH))


def _rel_bucket(n):
    max_exact = REL_BUCKETS // 2
    n = np.asarray(n)
    nl = np.maximum(n, max_exact).astype(np.float64)
    large = max_exact + (np.log(nl / max_exact) / math.log(REL_MAX_DISTANCE / max_exact)
                         * (REL_BUCKETS - max_exact)).astype(np.int32)
    large = np.minimum(large, REL_BUCKETS - 1)
    return np.where(n < max_exact, n, large).astype(np.int32)


def _attn_bias(rel_table, g, dilation, n_off):
    qi = np.arange(BLOCK)[:, None]
    ki = np.arange(2 * BLOCK)[None, :]
    off = BLOCK + qi - ki
    ok = (off >= 0) & (off <= n_off)
    buckets = _rel_bucket(np.clip(off, 0, n_off) * dilation)
    onehot = (buckets.reshape(1, -1) == np.arange(REL_BUCKETS)[:, None]).astype(np.float32)
    heads = rel_table[:, g * C_HEADS_PER_GROUP:(g + 1) * C_HEADS_PER_GROUP]
    bias = jnp.dot(heads.T.astype(F32), jnp.asarray(onehot), precision=lax.Precision.HIGHEST)
    bias = bias.reshape(C_HEADS_PER_GROUP, BLOCK, 2 * BLOCK)
    return jnp.where(ok[None], bias, NEG_MASK)


ATTN_SUBS_PER_STEP = 4


def _attn_kernel(q_ref, kp_ref, kc_ref, vp_ref, vc_ref, bias_ref, o_ref, lse_ref):
    first = pl.program_id(1) == 0
    key_is_prev = lax.broadcasted_iota(jnp.int32, (BLOCK, 2 * BLOCK), 1) < BLOCK
    hide = jnp.logical_and(first, key_is_prev)
    scale = C_HEAD_DIM ** -0.5
    for r in range(q_ref.shape[0]):
        for hd in range(C_HEADS_PER_GROUP):
            cs = slice(hd * C_HEAD_DIM, (hd + 1) * C_HEAD_DIM)
            keys = jnp.concatenate([kp_ref[r, :, cs], kc_ref[r, :, cs]], axis=0)
            vals = jnp.concatenate([vp_ref[r, :, cs], vc_ref[r, :, cs]], axis=0)
            logits = lax.dot_general(q_ref[r, :, cs], keys, (((1,), (1,)), ((), ())),
                                     preferred_element_type=F32) * scale
            logits = jnp.where(hide, NEG_MASK, logits + bias_ref[hd])
            m = jnp.max(logits, axis=-1, keepdims=True)
            e = jnp.exp(logits - m)
            s = jnp.sum(e, axis=-1, keepdims=True)
            o = jnp.dot(e.astype(BF16), vals, preferred_element_type=F32) / s
            o_ref[r, :, cs] = o.astype(o_ref.dtype)
            lse_ref[r, :, cs] = jnp.broadcast_to(m + jnp.log(s), (BLOCK, C_HEAD_DIM))


def dilated_attention_group(qkv, first_col, rel_table, g, batch, seq):
    window, dilation = C_PATTERNS[g]
    n_off = window // dilation
    sub_len = seq // dilation
    nb = sub_len // BLOCK
    subs = batch * dilation
    pv = qkv.reshape(subs, sub_len, qkv.shape[1])
    bias = _attn_bias(rel_table, g, dilation, n_off)
    c0 = first_col // C_OUT_WIDTH

    def col(k):
        return lambda s, n: (s, n, c0 + k)

    def col_prev(k):
        return lambda s, n: (s, jnp.maximum(n - 1, 0), c0 + k)

    per_step = math.gcd(subs, ATTN_SUBS_PER_STEP)
    blk = (per_step, BLOCK, C_OUT_WIDTH)
    out_blk = pl.BlockSpec(blk, lambda s, n: (s, n, 0))
    o, lse = pl.pallas_call(
        _attn_kernel,
        grid=(subs // per_step, nb),
        in_specs=[pl.BlockSpec(blk, col(0)),
                  pl.BlockSpec(blk, col_prev(1)), pl.BlockSpec(blk, col(1)),
                  pl.BlockSpec(blk, col_prev(2)), pl.BlockSpec(blk, col(2)),
                  _resident((C_HEADS_PER_GROUP, BLOCK, 2 * BLOCK))],
        out_specs=[out_blk, out_blk],
        out_shape=[jax.ShapeDtypeStruct((subs, sub_len, C_OUT_WIDTH), BF16),
                   jax.ShapeDtypeStruct((subs, sub_len, C_OUT_WIDTH), F32)],
        compiler_params=_params(2),
        name=f"dilated_attention_{g}",
    )(pv, pv, pv, pv, pv, bias)
    return o, lse


def _merge_kernel(x_ref, gate_ref, ya_ref, yb_ref, o0_ref, o1_ref, o2_ref, l0_ref, l1_ref, l2_ref,
                  wa_ref, wb_ref, wc_ref, wo_ref, g_ref, perm_ref, xo_ref, h_ref):
    tm = x_ref.shape[0]

    def token_order(ref, k, precision=None):
        return jnp.dot(perm_ref[k].astype(ref.dtype), ref[...].reshape(tm, C_OUT_WIDTH),
                       preferred_element_type=F32, precision=precision)

    exact_f32 = lax.Precision.HIGHEST
    l0, l1, l2 = l0_ref[...], token_order(l1_ref, 0, exact_f32), token_order(l2_ref, 1, exact_f32)
    m = jnp.maximum(jnp.maximum(l0, l1), l2)
    e0, e1, e2 = jnp.exp(l0 - m), jnp.exp(l1 - m), jnp.exp(l2 - m)
    yc = (e0 * o0_ref[...].astype(F32) + e1 * token_order(o1_ref, 0)
          + e2 * token_order(o2_ref, 1)) / (e0 + e1 + e2)

    def branch(k, y, w_ref):
        gate = jax.nn.sigmoid(gate_ref[:, k * D_MODEL:(k + 1) * D_MODEL].astype(F32))
        return gate * jnp.dot(y, w_ref[...], preferred_element_type=F32)

    merged = (branch(0, ya_ref[...], wa_ref) + branch(1, yb_ref[...], wb_ref)
              + branch(2, yc.astype(BF16), wc_ref))
    x = x_ref[...] + jnp.dot(merged.astype(BF16), wo_ref[...], preferred_element_type=F32)
    xo_ref[...] = x
    h_ref[...] = _rms(x, g_ref[...]).T.astype(h_ref.dtype)


def merge_and_project(x, p, ya, yb, outs, lses, wa, wb, wc, wo, g_ffn, batch, seq, tm=256):
    t, d = x.shape
    tiles = seq // tm
    row = lambda w: pl.BlockSpec((tm, w), lambda i: (i, 0))

    def grouped(dil):
        return pl.BlockSpec((None, dil, tm // dil, C_OUT_WIDTH), lambda i: (i // tiles, 0, i % tiles, 0))

    def by_residue(a, dil):
        return a.reshape(batch, dil, seq // dil, C_OUT_WIDTH)

    d1, d2 = DILATIONS
    perm = jnp.asarray(_group_by_residue_matrices(tm).transpose(0, 2, 1), BF16)
    return pl.pallas_call(
        _merge_kernel,
        grid=(t // tm,),
        in_specs=[row(d), row(GATE_WIDTH), row(A_WIDTH), row(B_WIDTH),
                  row(C_OUT_WIDTH), grouped(d1), grouped(d2),
                  row(C_OUT_WIDTH), grouped(d1), grouped(d2),
                  _resident((A_WIDTH, d)), _resident((B_WIDTH, d)), _resident((C_OUT_WIDTH, d)),
                  _resident((d, d)), _resident((1, d)), _resident(perm.shape)],
        out_specs=[row(d), pl.BlockSpec((d, tm), lambda i: (0, i))],
        out_shape=[jax.ShapeDtypeStruct((t, d), F32), jax.ShapeDtypeStruct((d, t), BF16)],
        compiler_params=_params(1),
        name="merge_and_project",
    )(x, p, ya, yb,
      outs[0].reshape(t, C_OUT_WIDTH), by_residue(outs[1], d1), by_residue(outs[2], d2),
      lses[0].reshape(t, C_OUT_WIDTH), by_residue(lses[1], d1), by_residue(lses[2], d2),
      wa, wb, wc, wo, g_ffn.reshape(1, d), perm)


def _candidate_pairs():
    return [(a, b) for a in range(PEER_TOPK) for b in range(PEER_TOPK // (a + 1))]


CAND_ROWS = 64


SUBLANES = 8


def _sorting_network(n):
    pairs, p = [], 1
    while p < n:
        k = p
        while k >= 1:
            for j in range(k % p, n - k, 2 * k):
                for i in range(min(k, n - j - k)):
                    if (i + j) // (2 * p) == (i + j + k) // (2 * p):
                        pairs.append((i + j, i + j + k))
            k //= 2
        p *= 2
    return pairs


def _top16_desc(slabs):
    lst = list(slabs)
    for a, b in _sorting_network(len(lst)):
        lst[a], lst[b] = jnp.maximum(lst[a], lst[b]), jnp.minimum(lst[a], lst[b])
    out = []
    for k in range(PEER_TOPK):
        head = jnp.max(lst[0], axis=0, keepdims=True)
        out.append(head)
        if k == PEER_TOPK - 1:
            break
        taken = lst[0] == head
        keep = min(len(lst), PEER_TOPK - k - 1)
        lst = [jnp.where(taken, lst[a + 1] if a + 1 < len(lst) else -jnp.inf, lst[a])
               for a in range(keep)]
    return out


def _row_slabs(x):
    return [x[r:r + SUBLANES, :] for r in range(0, x.shape[0], SUBLANES)]


COUNT_SPLIT_B = 4
COUNT_SPLIT_A = 3
assert (COUNT_SPLIT_A + 1) * (COUNT_SPLIT_B + 1) > PEER_TOPK


def _peer_select_kernel(h_ref, wq_ref, k1_ref, k2_ref, e2_ref, r2_ref, w_ref, n_ref,
                        cand_ref, v2_ref):
    tm = h_ref.shape[1]
    qt = jnp.dot(wq_ref[...], h_ref[...], preferred_element_type=F32)
    half = PEER_KEY_DIM // 2
    cand_ref[...] = jnp.full(cand_ref.shape, -jnp.inf, F32)
    for hd in range(PEER_HEADS):
        q1 = qt[hd * PEER_KEY_DIM:hd * PEER_KEY_DIM + half, :]
        q2 = qt[hd * PEER_KEY_DIM + half:(hd + 1) * PEER_KEY_DIM, :]
        s1 = jnp.dot(k1_ref[...], q1, preferred_element_type=F32, precision=lax.Precision.HIGHEST)
        s2 = jnp.dot(k2_ref[...], q2, preferred_element_type=F32, precision=lax.Precision.HIGHEST)
        v1 = _top16_desc(_row_slabs(s1))
        v2 = _top16_desc(_row_slabs(s2))
        for b in range(PEER_TOPK):
            v2_ref[b:b + 1, :] = v2[b]
        for idx, (a, b) in enumerate(_candidate_pairs()):
            cand_ref[idx:idx + 1, :] = v1[a] + v2[b]
        best = _top16_desc(_row_slabs(cand_ref[...]))
        tau = best[PEER_TOPK - 1]
        z = functools.reduce(lambda acc, c: acc + jnp.exp(c - best[0]), best[1:],
                             jnp.ones_like(best[0]))
        rank2 = jnp.zeros(s2.shape, F32)
        for b in range(PEER_TOPK):
            rank2 = jnp.where(v2[b] > s2, float(b + 1), rank2)
        count = jnp.zeros(s1.shape, F32)
        for b in range(COUNT_SPLIT_B):
            count = jnp.where(s1 + v2[b] >= tau, float(b + 1), count)
        v2_all = v2_ref[...]
        low_rank = lax.broadcasted_iota(jnp.int32, v2_all.shape, 0) >= COUNT_SPLIT_B
        for a in range(COUNT_SPLIT_A):
            reached = jnp.logical_and(v1[a] + v2_all >= tau, low_rank)
            extra = jnp.sum(jnp.where(reached, 1.0, 0.0), axis=0, keepdims=True)
            count = count + jnp.where(s1 == v1[a], extra, 0.0)
        e2_ref[hd] = jnp.exp(s2 - v2[0]).astype(e2_ref.dtype)
        r2_ref[hd] = rank2.astype(r2_ref.dtype)
        w_ref[hd] = jnp.exp(s1 - v1[0]) / z
        n_ref[hd] = count


def peer_select(h_t, wq_t, k1, k2, tm=256):
    d, t = h_t.shape
    tab = lambda dt: jax.ShapeDtypeStruct((PEER_HEADS, PEER_KEYS, t), dt)
    blk = pl.BlockSpec((PEER_HEADS, PEER_KEYS, tm), lambda i: (0, 0, i))
    return pl.pallas_call(
        _peer_select_kernel,
        grid=(t // tm,),
        in_specs=[pl.BlockSpec((d, tm), lambda i: (0, i)),
                  _resident((PEER_HEADS * PEER_KEY_DIM, d)),
                  _resident((PEER_KEYS, PEER_KEY_DIM // 2)),
                  _resident((PEER_KEYS, PEER_KEY_DIM // 2))],
        out_specs=[blk, blk, blk, blk],
        out_shape=[tab(BF16), tab(BF16), tab(F32), tab(F32)],
        scratch_shapes=[pltpu.VMEM((CAND_ROWS, tm), F32), pltpu.VMEM((PEER_TOPK, tm), F32)],
        compiler_params=_params(1),
        name="peer_select",
    )(h_t, wq_t, k1, k2)


def _peer_dense_kernel(h_ref, u_ref, vt_ref, e2_ref, r2_ref, w_ref, n_ref, x_ref, g_ref,
                       *rest, first_keys, final):
    if final:
        out_ref, acc_ref = rest
    else:
        perm_ref, xo_ref, hn_ref, *hd_refs, acc_ref = rest
    e = pl.program_id(1)
    tm = h_ref.shape[1]

    @pl.when(e == 0)
    def _():
        acc_ref[...] = jnp.zeros_like(acc_ref)

    gates = []
    for ii in range(first_keys):
        g = None
        for hd in range(PEER_HEADS):
            n_row = jnp.broadcast_to(n_ref[hd, ii:ii + 1, :].astype(BF16), (PEER_KEYS, tm))
            w_row = jnp.broadcast_to(w_ref[hd, ii:ii + 1, :].astype(BF16), (PEER_KEYS, tm))
            picked = jnp.where(r2_ref[hd] < n_row, e2_ref[hd], jnp.zeros((), BF16)) * w_row
            g = picked if g is None else g + picked
        gates.append(g)
    gate_t = jnp.concatenate(gates, axis=0)

    a_t = jnp.dot(u_ref[...], h_ref[...], preferred_element_type=F32)
    act_t = _gelu(a_t).astype(BF16) * gate_t
    acc_ref[...] += jnp.dot(vt_ref[...], act_t, preferred_element_type=F32)

    @pl.when(e == pl.num_programs(1) - 1)
    def _():
        x = x_ref[...] + acc_ref[...].T
        if final:
            out_ref[...] = _rms(x, g_ref[...])
        else:
            xo_ref[...] = x
            _write_norms(x, g_ref[...], perm_ref, hn_ref, hd_refs)


def peer_dense(h_t, u, v_t, e2, r2, w, n, x, g_next, final, batch, seq, tm=512, te=512):
    d, t = h_t.shape
    tiles = seq // tm
    first_keys = te // PEER_KEYS
    groups = PEER_KEYS // first_keys
    w4 = w.reshape(PEER_HEADS, groups, first_keys, t)
    n4 = n.reshape(PEER_HEADS, groups, first_keys, t)
    n_tiles = PEER_N_EXPERTS // te
    tab = pl.BlockSpec((PEER_HEADS, PEER_KEYS, tm), lambda i, e: (0, 0, i))
    rows = pl.BlockSpec((PEER_HEADS, None, first_keys, tm), lambda i, e: (0, e, 0, i))
    tok = pl.BlockSpec((tm, d), lambda i, e: (i, 0))
    operands = [h_t, u, v_t, e2, r2, w4, n4, x, g_next.reshape(1, d)]
    in_specs = [pl.BlockSpec((d, tm), lambda i, e: (0, i)),
                pl.BlockSpec((te, d), lambda i, e: (e, 0)),
                pl.BlockSpec((d, te), lambda i, e: (0, e)),
                tab, tab, rows, rows, tok, _resident((1, d))]
    if final:
        out_specs = tok
        out_shape = jax.ShapeDtypeStruct((t, d), F32)
    else:
        specs, shapes = _norm_specs(batch, seq, d, tm, lambda i, e: (i // tiles, i % tiles))
        out_specs = [tok] + specs
        out_shape = [jax.ShapeDtypeStruct((t, d), F32)] + shapes
        perm = jnp.asarray(_group_by_residue_matrices(tm), BF16)
        operands.append(perm)
        in_specs.append(_resident(perm.shape))
    return pl.pallas_call(
        functools.partial(_peer_dense_kernel, first_keys=first_keys, final=final),
        grid=(t // tm, n_tiles),
        in_specs=in_specs,
        out_specs=out_specs,
        out_shape=out_shape,
        scratch_shapes=[pltpu.VMEM((d, tm), F32)],
        compiler_params=_params(2),
        name="peer_dense_final" if final else "peer_dense",
    )(*operands)


def _split_w_in(w):
    cuts = np.cumsum([0, A_WIDTH, A_WIDTH, B_WIDTH, C_QKV_WIDTH, C_QKV_WIDTH, C_QKV_WIDTH, GATE_WIDTH])
    au, av, b, q, k, v, gates = (w[:, lo:hi] for lo, hi in zip(cuts[:-1], cuts[1:]))
    group = lambda g: [m[:, g * C_OUT_WIDTH:(g + 1) * C_OUT_WIDTH] for m in (q, k, v)]
    main = jnp.concatenate([gates, au, av, b] + group(0), axis=1).astype(BF16)
    return main, [jnp.concatenate(group(g), axis=1).astype(BF16) for g in range(1, C_GROUPS)]


def kernel(x, rel_bias_table, norm_mix_g, w_in, a_norm_g, a_w_s, a_b_s, b_w_pool, b_scale, w_br_a,
           w_br_b, w_br_c, w_out, norm_ffn_g, peer_w_q, peer_sub_keys_1, peer_sub_keys_2, peer_u,
           peer_v, final_norm_g):
    batch, seq, d = x.shape
    t = batch * seq
    xf = x.reshape(t, d)
    hs = rmsnorm_rows(xf, norm_mix_g[0], batch, seq)
    out = None
    for l in range(DEPTH):
        w_main, w_groups = _split_w_in(w_in[l])
        p = in_proj(hs[0], w_main)
        ya = mixer_a(p, a_norm_g[l], a_w_s[l], a_b_s[l])
        yb = mixer_b(p, b_w_pool[l].astype(BF16), b_scale[l], seq)
        outs, lses = [], []
        for g in range(C_GROUPS):
            if g == 0:
                qkv, first_col = p, COL_QKV0
            else:
                qkv, first_col = in_proj(hs[g].reshape(t, d), w_groups[g - 1], tn=QKV_GROUP_WIDTH), 0
            o, lse = dilated_attention_group(qkv, first_col, rel_bias_table, g, batch, seq)
            outs.append(o)
            lses.append(lse)
        x_mid, h_ffn = merge_and_project(
            xf, p, ya, yb, outs, lses, w_br_a[l].astype(BF16), w_br_b[l].astype(BF16),
            w_br_c[l].astype(BF16), w_out[l].astype(BF16), norm_ffn_g[l], batch, seq)
        e2, r2, w, n = peer_select(h_ffn, peer_w_q[l].T.astype(BF16), peer_sub_keys_1[l],
                                   peer_sub_keys_2[l])
        final = l == DEPTH - 1
        g_next = final_norm_g if final else norm_mix_g[l + 1]
        res = peer_dense(h_ffn, peer_u[l].astype(BF16), peer_v[l].T.astype(BF16), e2, r2, w, n,
                         x_mid, g_next, final, batch, seq)
        if final:
            out = res
        else:
            xf, hs = res[0], res[1:]
    return out.reshape(batch, seq, d)
```

```python
import functools
import math

import numpy as np
import jax
import jax.numpy as jnp
from jax import lax
from jax.experimental import pallas as pl
from jax.experimental.pallas import tpu as pltpu

D_MODEL = 2048
DEPTH = 2
BLOCK = 128
EPS = 1e-6
A_HEADS = 8
A_HEAD_DIM = 128
A_WIDTH = A_HEADS * A_HEAD_DIM
POOL_WINDOWS = (2, 4, 8, 16)
B_GROUP_DIM = 256
B_WIDTH = len(POOL_WINDOWS) * B_GROUP_DIM
C_PATTERNS = ((128, 1), (512, 4), (2048, 16))
C_GROUPS = len(C_PATTERNS)
C_HEADS_PER_GROUP = 4
C_HEAD_DIM = 128
C_HEADS = C_GROUPS * C_HEADS_PER_GROUP
C_QKV_WIDTH = C_HEADS * C_HEAD_DIM
C_OUT_WIDTH = C_HEADS_PER_GROUP * C_HEAD_DIM
N_BRANCHES = 3
GATE_WIDTH = N_BRANCHES * D_MODEL
IN_WIDTH = 2 * A_WIDTH + B_WIDTH + 3 * C_QKV_WIDTH + GATE_WIDTH
REL_BUCKETS = 32
REL_MAX_DISTANCE = 2048
PEER_HEADS = 8
PEER_KEYS = 128
PEER_N_EXPERTS = PEER_KEYS * PEER_KEYS
PEER_KEY_DIM = 256
PEER_TOPK = 16

COL_GATE = 0
COL_AU = GATE_WIDTH
COL_AV = COL_AU + A_WIDTH
COL_B = COL_AV + A_WIDTH
COL_QKV0 = COL_B + B_WIDTH
QKV_GROUP_WIDTH = 3 * C_OUT_WIDTH
MAIN_WIDTH = COL_QKV0 + QKV_GROUP_WIDTH

NEG_MASK = -1e30
BF16 = jnp.bfloat16
F32 = jnp.float32

VMEM_LIMIT_BYTES = 56 * 1024 * 1024
LANES = 128
PACKED_ROWS = 16
MXU_COLS = 256


def _params(n_axes):
    return pltpu.CompilerParams(
        dimension_semantics=("arbitrary",) * n_axes, vmem_limit_bytes=VMEM_LIMIT_BYTES)


def _rms(x, g):
    return x * lax.rsqrt(jnp.mean(x * x, axis=-1, keepdims=True) + EPS) * g


def _gelu(x):
    return 0.5 * x * (1.0 + lax.erf(x * (1.0 / math.sqrt(2.0))))


def _resident(shape):
    nd = len(shape)
    return pl.BlockSpec(shape, lambda *_: (0,) * nd, pipeline_mode=pl.Buffered(1))


DILATIONS = tuple(d for _, d in C_PATTERNS if d > 1)


def _group_by_residue_matrices(tm):
    mats = []
    for dil in DILATIONS:
        dst = np.arange(tm)
        src = (dst % (tm // dil)) * dil + dst // (tm // dil)
        mats.append((src[:, None] == np.arange(tm)[None, :]).astype(np.float32))
    return np.stack(mats)


def _write_norms(x, g, perm_ref, h_ref, hd_refs):
    h = _rms(x, g).astype(h_ref.dtype)
    h_ref[...] = h
    for k, ref in enumerate(hd_refs):
        grouped = jnp.dot(perm_ref[k], h, preferred_element_type=F32).astype(ref.dtype)
        ref[...] = grouped.reshape(ref.shape)


def _norm_specs(batch, seq, d, tm, index, buffering=None):
    specs = [pl.BlockSpec((tm, d), lambda *ids: (index(*ids)[0] * (seq // tm) + index(*ids)[1], 0),
                          pipeline_mode=buffering)]
    shapes = [jax.ShapeDtypeStruct((batch * seq, d), BF16)]
    for dil in DILATIONS:
        specs.append(pl.BlockSpec((None, dil, tm // dil, d),
                                  lambda *ids: (index(*ids)[0], 0, index(*ids)[1], 0),
                                  pipeline_mode=buffering))
        shapes.append(jax.ShapeDtypeStruct((batch, dil, seq // dil, d), BF16))
    return specs, shapes


def _norm_kernel(x_ref, g_ref, perm_ref, h_ref, *hd_refs):
    _write_norms(x_ref[...], g_ref[...], perm_ref, h_ref, hd_refs)


def rmsnorm_rows(x, g, batch, seq, tm=512):
    t, d = x.shape
    tiles = seq // tm
    specs, shapes = _norm_specs(batch, seq, d, tm, lambda i: (i // tiles, i % tiles))
    perm = jnp.asarray(_group_by_residue_matrices(tm), BF16)
    return pl.pallas_call(
        _norm_kernel,
        grid=(t // tm,),
        in_specs=[pl.BlockSpec((tm, d), lambda i: (i, 0)), _resident((1, d)), _resident(perm.shape)],
        out_specs=specs,
        out_shape=shapes,
        compiler_params=_params(1),
        name="rmsnorm_rows",
    )(x, g.reshape(1, d), perm)


def _matmul_kernel(h_ref, w_ref, o_ref):
    o_ref[...] = jnp.dot(h_ref[...], w_ref[...], preferred_element_type=F32).astype(o_ref.dtype)


def in_proj(h, w, tm=1024, tn=1536):
    t, d = h.shape
    n = w.shape[1]
    return pl.pallas_call(
        _matmul_kernel,
        grid=(n // tn, t // tm),
        in_specs=[pl.BlockSpec((tm, d), lambda j, i: (i, 0)),
                  pl.BlockSpec((d, tn), lambda j, i: (0, j))],
        out_specs=pl.BlockSpec((tm, tn), lambda j, i: (i, j)),
        out_shape=jax.ShapeDtypeStruct((t, n), BF16),
        compiler_params=_params(2),
        name="in_proj",
    )(h, w)


def _mixer_a_kernel(au_ref, av_ref, g_ref, ws_ref, bs_ref, o_ref, *, chunks):
    u = _gelu(au_ref[...].astype(F32))
    v = _rms(_gelu(av_ref[...].astype(F32)), g_ref[...]).astype(BF16)
    row = lax.broadcasted_iota(jnp.int32, (BLOCK, BLOCK), 0)
    col = lax.broadcasted_iota(jnp.int32, (BLOCK, BLOCK), 1)
    causal = col <= row
    for hd in range(A_HEADS):
        w = jnp.where(causal, ws_ref[hd], 0.0).astype(BF16)
        bias = bs_ref[hd]
        cs = slice(hd * A_HEAD_DIM, (hd + 1) * A_HEAD_DIM)
        for c in range(chunks):
            rs = slice(c * BLOCK, (c + 1) * BLOCK)
            s = jnp.dot(w, v[rs, cs], preferred_element_type=F32) + bias
            o_ref[rs, cs] = (u[rs, cs] * s).astype(o_ref.dtype)


def mixer_a(p, a_norm_g, a_w_s, a_b_s, chunks=4):
    t = p.shape[0]
    tm = chunks * BLOCK
    bias = jnp.broadcast_to(a_b_s[:, :, None], (A_HEADS, BLOCK, A_HEAD_DIM))
    return pl.pallas_call(
        functools.partial(_mixer_a_kernel, chunks=chunks),
        grid=(t // tm,),
        in_specs=[pl.BlockSpec((tm, A_WIDTH), lambda i: (i, COL_AU // A_WIDTH)),
                  pl.BlockSpec((tm, A_WIDTH), lambda i: (i, COL_AV // A_WIDTH)),
                  _resident((1, A_WIDTH)),
                  _resident((A_HEADS, BLOCK, BLOCK)),
                  _resident((A_HEADS, BLOCK, A_HEAD_DIM))],
        out_specs=pl.BlockSpec((tm, A_WIDTH), lambda i: (i, 0)),
        out_shape=jax.ShapeDtypeStruct((t, A_WIDTH), BF16),
        compiler_params=_params(1),
        name="mixer_a",
    )(p, p, a_norm_g.reshape(1, A_WIDTH), a_w_s, bias)


def _pool_band_matrices():
    t = np.arange(BLOCK)[:, None]
    k = np.arange(2 * BLOCK)[None, :]
    back = BLOCK + t - k
    return np.stack([((back >= 0) & (back < w)) for w in POOL_WINDOWS]).astype(np.float32)


def _mixer_b_kernel(prev_ref, cur_ref, band_ref, wp_ref, sc_ref, o_ref, *, chunks, tiles_per_seq):
    first = (pl.program_id(0) % tiles_per_seq) == 0
    prev = jnp.where(first, jnp.zeros_like(prev_ref[...]), prev_ref[...])
    tile_pos = (pl.program_id(0) % tiles_per_seq) * (chunks * BLOCK)
    for c in range(chunks):
        rs = slice(c * BLOCK, (c + 1) * BLOCK)
        before = prev if c == 0 else cur_ref[(c - 1) * BLOCK:c * BLOCK, :]
        cur = cur_ref[rs, :]
        both = jnp.concatenate([before, cur], axis=0)
        pos = tile_pos + c * BLOCK + lax.broadcasted_iota(jnp.int32, (BLOCK, B_GROUP_DIM), 0)
        for g, w in enumerate(POOL_WINDOWS):
            cs = slice(g * B_GROUP_DIM, (g + 1) * B_GROUP_DIM)
            wsum = jnp.dot(band_ref[g], both[:, cs], preferred_element_type=F32)
            count = jnp.minimum(pos + 1, w).astype(F32)
            diff = (wsum / count - cur[:, cs].astype(F32)).astype(BF16)
            y = jnp.dot(diff, wp_ref[g], preferred_element_type=F32) * sc_ref[:, cs]
            o_ref[rs, cs] = y.astype(o_ref.dtype)


def mixer_b(p, w_pool, scale, seq, chunks=4):
    t = p.shape[0]
    tm = chunks * BLOCK
    band = jnp.asarray(_pool_band_matrices(), BF16)
    col = COL_B // B_WIDTH
    return pl.pallas_call(
        functools.partial(_mixer_b_kernel, chunks=chunks, tiles_per_seq=seq // tm),
        grid=(t // tm,),
        in_specs=[pl.BlockSpec((BLOCK, B_WIDTH), lambda i: (jnp.maximum(i * chunks - 1, 0), col)),
                  pl.BlockSpec((tm, B_WIDTH), lambda i: (i, col)),
                  _resident((len(POOL_WINDOWS), BLOCK, 2 * BLOCK)),
                  _resident((len(POOL_WINDOWS), B_GROUP_DIM, B_GROUP_DIM)),
                  _resident((1, B_WIDTH))],
        out_specs=pl.BlockSpec((tm, B_WIDTH), lambda i: (i, 0)),
        out_shape=jax.ShapeDtypeStruct((t, B_WIDTH), BF16),
        compiler_params=_params(1),
        name="mixer_b",
    )(p, p, band, w_pool, scale.reshape(1, B_WIDTH))


def _rel_bucket(n):
    max_exact = REL_BUCKETS // 2
    n = np.asarray(n)
    nl = np.maximum(n, max_exact).astype(np.float64)
    large = max_exact + (np.log(nl / max_exact) / math.log(REL_MAX_DISTANCE / max_exact)
                         * (REL_BUCKETS - max_exact)).astype(np.int32)
    large = np.minimum(large, REL_BUCKETS - 1)
    return np.where(n < max_exact, n, large).astype(np.int32)


def _attn_bias(rel_table, g, dilation, n_off):
    qi = np.arange(BLOCK)[:, None]
    ki = np.arange(2 * BLOCK)[None, :]
    off = BLOCK + qi - ki
    ok = (off >= 0) & (off <= n_off)
    buckets = _rel_bucket(np.clip(off, 0, n_off) * dilation)
    onehot = (buckets.reshape(1, -1) == np.arange(REL_BUCKETS)[:, None]).astype(np.float32)
    heads = rel_table[:, g * C_HEADS_PER_GROUP:(g + 1) * C_HEADS_PER_GROUP]
    bias = jnp.dot(heads.T.astype(F32), jnp.asarray(onehot), precision=lax.Precision.HIGHEST)
    bias = bias.reshape(C_HEADS_PER_GROUP, BLOCK, 2 * BLOCK)
    return jnp.where(ok[None], bias, NEG_MASK)


ATTN_SUBS_PER_STEP = 4


def _attn_kernel(q_ref, kp_ref, kc_ref, vp_ref, vc_ref, bias_ref, o_ref, lse_ref):
    first = pl.program_id(1) == 0
    key_is_prev = lax.broadcasted_iota(jnp.int32, (BLOCK, 2 * BLOCK), 1) < BLOCK
    hide = jnp.logical_and(first, key_is_prev)
    scale = C_HEAD_DIM ** -0.5
    for r in range(q_ref.shape[0]):
        for hd in range(C_HEADS_PER_GROUP):
            cs = slice(hd * C_HEAD_DIM, (hd + 1) * C_HEAD_DIM)
            keys = jnp.concatenate([kp_ref[r, :, cs], kc_ref[r, :, cs]], axis=0)
            vals = jnp.concatenate([vp_ref[r, :, cs], vc_ref[r, :, cs]], axis=0)
            logits = lax.dot_general(q_ref[r, :, cs], keys, (((1,), (1,)), ((), ())),
                                     preferred_element_type=F32) * scale
            logits = jnp.where(hide, NEG_MASK, logits + bias_ref[hd])
            m = jnp.max(logits, axis=-1, keepdims=True)
            e = jnp.exp(logits - m)
            s = jnp.sum(e, axis=-1, keepdims=True)
            o = jnp.dot(e.astype(BF16), vals, preferred_element_type=F32) / s
            o_ref[r, :, cs] = o.astype(o_ref.dtype)
            lse_ref[r, :, cs] = jnp.broadcast_to(m + jnp.log(s), (BLOCK, C_HEAD_DIM))


def dilated_attention_group(qkv, first_col, rel_table, g, batch, seq):
    window, dilation = C_PATTERNS[g]
    n_off = window // dilation
    sub_len = seq // dilation
    nb = sub_len // BLOCK
    subs = batch * dilation
    pv = qkv.reshape(subs, sub_len, qkv.shape[1])
    bias = _attn_bias(rel_table, g, dilation, n_off)
    c0 = first_col // C_OUT_WIDTH

    def col(k):
        return lambda s, n: (s, n, c0 + k)

    def col_prev(k):
        return lambda s, n: (s, jnp.maximum(n - 1, 0), c0 + k)

    per_step = math.gcd(subs, ATTN_SUBS_PER_STEP)
    blk = (per_step, BLOCK, C_OUT_WIDTH)
    out_blk = pl.BlockSpec(blk, lambda s, n: (s, n, 0))
    o, lse = pl.pallas_call(
        _attn_kernel,
        grid=(subs // per_step, nb),
        in_specs=[pl.BlockSpec(blk, col(0)),
                  pl.BlockSpec(blk, col_prev(1)), pl.BlockSpec(blk, col(1)),
                  pl.BlockSpec(blk, col_prev(2)), pl.BlockSpec(blk, col(2)),
                  _resident((C_HEADS_PER_GROUP, BLOCK, 2 * BLOCK))],
        out_specs=[out_blk, out_blk],
        out_shape=[jax.ShapeDtypeStruct((subs, sub_len, C_OUT_WIDTH), BF16),
                   jax.ShapeDtypeStruct((subs, sub_len, C_OUT_WIDTH), F32)],
        compiler_params=_params(2),
        name=f"dilated_attention_{g}",
    )(pv, pv, pv, pv, pv, bias)
    return o, lse


def _merge_kernel(x_ref, gate_ref, ya_ref, yb_ref, o0_ref, o1_ref, o2_ref, l0_ref, l1_ref, l2_ref,
                  wa_ref, wb_ref, wc_ref, wo_ref, g_ref, perm_ref, xo_ref, h_ref):
    tm = x_ref.shape[0]

    def token_order(ref, k, precision=None):
        return jnp.dot(perm_ref[k].astype(ref.dtype), ref[...].reshape(tm, C_OUT_WIDTH),
                       preferred_element_type=F32, precision=precision)

    exact_f32 = lax.Precision.HIGHEST
    l0, l1, l2 = l0_ref[...], token_order(l1_ref, 0, exact_f32), token_order(l2_ref, 1, exact_f32)
    m = jnp.maximum(jnp.maximum(l0, l1), l2)
    e0, e1, e2 = jnp.exp(l0 - m), jnp.exp(l1 - m), jnp.exp(l2 - m)
    yc = (e0 * o0_ref[...].astype(F32) + e1 * token_order(o1_ref, 0)
          + e2 * token_order(o2_ref, 1)) / (e0 + e1 + e2)

    def branch(k, y, w_ref):
        gate = jax.nn.sigmoid(gate_ref[:, k * D_MODEL:(k + 1) * D_MODEL].astype(F32))
        return gate * jnp.dot(y, w_ref[...], preferred_element_type=F32)

    merged = (branch(0, ya_ref[...], wa_ref) + branch(1, yb_ref[...], wb_ref)
              + branch(2, yc.astype(BF16), wc_ref))
    x = x_ref[...] + jnp.dot(merged.astype(BF16), wo_ref[...], preferred_element_type=F32)
    xo_ref[...] = x
    h_ref[...] = _rms(x, g_ref[...]).T.astype(h_ref.dtype)


def merge_and_project(x, p, ya, yb, outs, lses, wa, wb, wc, wo, g_ffn, batch, seq, tm=256):
    t, d = x.shape
    tiles = seq // tm
    row = lambda w: pl.BlockSpec((tm, w), lambda i: (i, 0))

    def grouped(dil):
        return pl.BlockSpec((None, dil, tm // dil, C_OUT_WIDTH), lambda i: (i // tiles, 0, i % tiles, 0))

    def by_residue(a, dil):
        return a.reshape(batch, dil, seq // dil, C_OUT_WIDTH)

    d1, d2 = DILATIONS
    perm = jnp.asarray(_group_by_residue_matrices(tm).transpose(0, 2, 1), BF16)
    return pl.pallas_call(
        _merge_kernel,
        grid=(t // tm,),
        in_specs=[row(d), row(GATE_WIDTH), row(A_WIDTH), row(B_WIDTH),
                  row(C_OUT_WIDTH), grouped(d1), grouped(d2),
                  row(C_OUT_WIDTH), grouped(d1), grouped(d2),
                  _resident((A_WIDTH, d)), _resident((B_WIDTH, d)), _resident((C_OUT_WIDTH, d)),
                  _resident((d, d)), _resident((1, d)), _resident(perm.shape)],
        out_specs=[row(d), pl.BlockSpec((d, tm), lambda i: (0, i))],
        out_shape=[jax.ShapeDtypeStruct((t, d), F32), jax.ShapeDtypeStruct((d, t), BF16)],
        compiler_params=_params(1),
        name="merge_and_project",
    )(x, p, ya, yb,
      outs[0].reshape(t, C_OUT_WIDTH), by_residue(outs[1], d1), by_residue(outs[2], d2),
      lses[0].reshape(t, C_OUT_WIDTH), by_residue(lses[1], d1), by_residue(lses[2], d2),
      wa, wb, wc, wo, g_ffn.reshape(1, d), perm)


def _candidate_pairs():
    return [(a, b) for a in range(PEER_TOPK) for b in range(PEER_TOPK // (a + 1))]


CAND_ROWS = 64


SUBLANES = 8


def _sorting_network(n):
    pairs, p = [], 1
    while p < n:
        k = p
        while k >= 1:
            for j in range(k % p, n - k, 2 * k):
                for i in range(min(k, n - j - k)):
                    if (i + j) // (2 * p) == (i + j + k) // (2 * p):
                        pairs.append((i + j, i + j + k))
            k //= 2
        p *= 2
    return pairs


def _top16_desc(slabs):
    lst = list(slabs)
    for a, b in _sorting_network(len(lst)):
        lst[a], lst[b] = jnp.maximum(lst[a], lst[b]), jnp.minimum(lst[a], lst[b])
    out = []
    for k in range(PEER_TOPK):
        head = jnp.max(lst[0], axis=0, keepdims=True)
        out.append(head)
        if k == PEER_TOPK - 1:
            break
        taken = lst[0] == head
        keep = min(len(lst), PEER_TOPK - k - 1)
        lst = [jnp.where(taken, lst[a + 1] if a + 1 < len(lst) else -jnp.inf, lst[a])
               for a in range(keep)]
    return out


def _row_slabs(x):
    return [x[r:r + SUBLANES, :] for r in range(0, x.shape[0], SUBLANES)]


COUNT_SPLIT_B = 4
COUNT_SPLIT_A = 3
assert (COUNT_SPLIT_A + 1) * (COUNT_SPLIT_B + 1) > PEER_TOPK


def _peer_select_kernel(h_ref, wq_ref, k1_ref, k2_ref, e2_ref, r2_ref, w_ref, n_ref,
                        cand_ref, v2_ref):
    tm = h_ref.shape[1]
    qt = jnp.dot(wq_ref[...], h_ref[...], preferred_element_type=F32)
    half = PEER_KEY_DIM // 2
    cand_ref[...] = jnp.full(cand_ref.shape, -jnp.inf, F32)
    for hd in range(PEER_HEADS):
        q1 = qt[hd * PEER_KEY_DIM:hd * PEER_KEY_DIM + half, :]
        q2 = qt[hd * PEER_KEY_DIM + half:(hd + 1) * PEER_KEY_DIM, :]
        s1 = jnp.dot(k1_ref[...], q1, preferred_element_type=F32, precision=lax.Precision.HIGHEST)
        s2 = jnp.dot(k2_ref[...], q2, preferred_element_type=F32, precision=lax.Precision.HIGHEST)
        v1 = _top16_desc(_row_slabs(s1))
        v2 = _top16_desc(_row_slabs(s2))
        for b in range(PEER_TOPK):
            v2_ref[b:b + 1, :] = v2[b]
        for idx, (a, b) in enumerate(_candidate_pairs()):
            cand_ref[idx:idx + 1, :] = v1[a] + v2[b]
        best = _top16_desc(_row_slabs(cand_ref[...]))
        tau = best[PEER_TOPK - 1]
        z = functools.reduce(lambda acc, c: acc + jnp.exp(c - best[0]), best[1:],
                             jnp.ones_like(best[0]))
        rank2 = jnp.zeros(s2.shape, F32)
        for b in range(PEER_TOPK):
            rank2 = jnp.where(v2[b] > s2, float(b + 1), rank2)
        count = jnp.zeros(s1.shape, F32)
        for b in range(COUNT_SPLIT_B):
            count = jnp.where(s1 + v2[b] >= tau, float(b + 1), count)
        v2_all = v2_ref[...]
        low_rank = lax.broadcasted_iota(jnp.int32, v2_all.shape, 0) >= COUNT_SPLIT_B
        for a in range(COUNT_SPLIT_A):
            reached = jnp.logical_and(v1[a] + v2_all >= tau, low_rank)
            extra = jnp.sum(jnp.where(reached, 1.0, 0.0), axis=0, keepdims=True)
            count = count + jnp.where(s1 == v1[a], extra, 0.0)
        e2_ref[hd] = jnp.exp(s2 - v2[0]).astype(e2_ref.dtype)
        r2_ref[hd] = rank2.astype(r2_ref.dtype)
        w_ref[hd] = jnp.exp(s1 - v1[0]) / z
        n_ref[hd] = count


def peer_select(h_t, wq_t, k1, k2, tm=512):
    d, t = h_t.shape
    tab = lambda dt: jax.ShapeDtypeStruct((PEER_HEADS, PEER_KEYS, t), dt)
    blk = pl.BlockSpec((PEER_HEADS, PEER_KEYS, tm), lambda i: (0, 0, i))
    return pl.pallas_call(
        _peer_select_kernel,
        grid=(t // tm,),
        in_specs=[pl.BlockSpec((d, tm), lambda i: (0, i)),
                  _resident((PEER_HEADS * PEER_KEY_DIM, d)),
                  _resident((PEER_KEYS, PEER_KEY_DIM // 2)),
                  _resident((PEER_KEYS, PEER_KEY_DIM // 2))],
        out_specs=[blk, blk, blk, blk],
        out_shape=[tab(BF16), tab(BF16), tab(F32), tab(F32)],
        scratch_shapes=[pltpu.VMEM((CAND_ROWS, tm), F32), pltpu.VMEM((PEER_TOPK, tm), F32)],
        compiler_params=_params(1),
        name="peer_select",
    )(h_t, wq_t, k1, k2)


def _peer_dense_kernel(h_ref, u_ref, vt_ref, e2_ref, r2_ref, w_ref, n_ref, x_ref, g_ref,
                       *rest, first_keys, final):
    if final:
        out_ref, acc_ref = rest
    else:
        perm_ref, xo_ref, hn_ref, *hd_refs, acc_ref = rest
    e = pl.program_id(1)
    tm = h_ref.shape[1]

    @pl.when(e == 0)
    def _():
        acc_ref[...] = jnp.zeros_like(acc_ref)

    first_key = e * first_keys
    gates = []
    for ii in range(first_keys):
        g = None
        for hd in range(PEER_HEADS):
            n_row = jnp.broadcast_to(n_ref[hd, pl.ds(first_key + ii, 1), :].astype(BF16), (PEER_KEYS, tm))
            w_row = jnp.broadcast_to(w_ref[hd, pl.ds(first_key + ii, 1), :].astype(BF16), (PEER_KEYS, tm))
            picked = jnp.where(r2_ref[hd] < n_row, e2_ref[hd], jnp.zeros((), BF16)) * w_row
            g = picked if g is None else g + picked
        gates.append(g)
    gate_t = jnp.concatenate(gates, axis=0)

    a_t = jnp.dot(u_ref[...], h_ref[...], preferred_element_type=F32)
    act_t = _gelu(a_t).astype(BF16) * gate_t
    acc_ref[...] += jnp.dot(vt_ref[...], act_t, preferred_element_type=F32)

    @pl.when(e == pl.num_programs(1) - 1)
    def _():
        x = x_ref[...] + acc_ref[...].T
        if final:
            out_ref[...] = _rms(x, g_ref[...])
        else:
            xo_ref[...] = x
            _write_norms(x, g_ref[...], perm_ref, hn_ref, hd_refs)


def peer_dense(h_t, u, v_t, e2, r2, w, n, x, g_next, final, batch, seq, tm=512, te=1024):
    d, t = h_t.shape
    tiles = seq // tm
    first_keys = te // PEER_KEYS
    n_tiles = PEER_N_EXPERTS // te
    tab = pl.BlockSpec((PEER_HEADS, PEER_KEYS, tm), lambda i, e: (0, 0, i), pipeline_mode=pl.Buffered(1))
    tok = pl.BlockSpec((tm, d), lambda i, e: (i, 0))
    operands = [h_t, u, v_t, e2, r2, w, n, x, g_next.reshape(1, d)]
    in_specs = [pl.BlockSpec((d, tm), lambda i, e: (0, i)),
                pl.BlockSpec((te, d), lambda i, e: (e, 0)),
                pl.BlockSpec((d, te), lambda i, e: (0, e)),
                tab, tab, tab, tab,
                pl.BlockSpec((tm, d), lambda i, e: (i, 0), pipeline_mode=pl.Buffered(1)),
                _resident((1, d))]
    once = pl.Buffered(1)
    tok_out = pl.BlockSpec((tm, d), lambda i, e: (i, 0), pipeline_mode=once)
    if final:
        out_specs = tok_out
        out_shape = jax.ShapeDtypeStruct((t, d), F32)
    else:
        specs, shapes = _norm_specs(batch, seq, d, tm, lambda i, e: (i // tiles, i % tiles), once)
        out_specs = [tok_out] + specs
        out_shape = [jax.ShapeDtypeStruct((t, d), F32)] + shapes
        perm = jnp.asarray(_group_by_residue_matrices(tm), BF16)
        operands.append(perm)
        in_specs.append(_resident(perm.shape))
    return pl.pallas_call(
        functools.partial(_peer_dense_kernel, first_keys=first_keys, final=final),
        grid=(t // tm, n_tiles),
        in_specs=in_specs,
        out_specs=out_specs,
        out_shape=out_shape,
        scratch_shapes=[pltpu.VMEM((d, tm), F32)],
        compiler_params=_params(2),
        name="peer_dense_final" if final else "peer_dense",
    )(*operands)


def _split_w_in(w):
    cuts = np.cumsum([0, A_WIDTH, A_WIDTH, B_WIDTH, C_QKV_WIDTH, C_QKV_WIDTH, C_QKV_WIDTH, GATE_WIDTH])
    au, av, b, q, k, v, gates = (w[:, lo:hi] for lo, hi in zip(cuts[:-1], cuts[1:]))
    group = lambda g: [m[:, g * C_OUT_WIDTH:(g + 1) * C_OUT_WIDTH] for m in (q, k, v)]
    main = jnp.concatenate([gates, au, av, b] + group(0), axis=1).astype(BF16)
    return main, [jnp.concatenate(group(g), axis=1).astype(BF16) for g in range(1, C_GROUPS)]


def kernel(x, rel_bias_table, norm_mix_g, w_in, a_norm_g, a_w_s, a_b_s, b_w_pool, b_scale, w_br_a,
           w_br_b, w_br_c, w_out, norm_ffn_g, peer_w_q, peer_sub_keys_1, peer_sub_keys_2, peer_u,
           peer_v, final_norm_g):
    batch, seq, d = x.shape
    t = batch * seq
    xf = x.reshape(t, d)
    hs = rmsnorm_rows(xf, norm_mix_g[0], batch, seq)
    out = None
    for l in range(DEPTH):
        w_main, w_groups = _split_w_in(w_in[l])
        p = in_proj(hs[0], w_main)
        ya = mixer_a(p, a_norm_g[l], a_w_s[l], a_b_s[l])
        yb = mixer_b(p, b_w_pool[l].astype(BF16), b_scale[l], seq)
        outs, lses = [], []
        for g in range(C_GROUPS):
            if g == 0:
                qkv, first_col = p, COL_QKV0
            else:
                qkv, first_col = in_proj(hs[g].reshape(t, d), w_groups[g - 1], tn=QKV_GROUP_WIDTH), 0
            o, lse = dilated_attention_group(qkv, first_col, rel_bias_table, g, batch, seq)
            outs.append(o)
            lses.append(lse)
        x_mid, h_ffn = merge_and_project(
            xf, p, ya, yb, outs, lses, w_br_a[l].astype(BF16), w_br_b[l].astype(BF16),
            w_br_c[l].astype(BF16), w_out[l].astype(BF16), norm_ffn_g[l], batch, seq)
        e2, r2, w, n = peer_select(h_ffn, peer_w_q[l].T.astype(BF16), peer_sub_keys_1[l],
                                   peer_sub_keys_2[l])
        final = l == DEPTH - 1
        g_next = final_norm_g if final else norm_mix_g[l + 1]
        res = peer_dense(h_ffn, peer_u[l].astype(BF16), peer_v[l].T.astype(BF16), e2, r2, w, n,
                         x_mid, g_next, final, batch, seq)
        if final:
            out = res
        else:
            xf, hs = res[0], res[1:]
    return out.reshape(batch, seq, d)
```

```python
import functools
import math

import numpy as np
import jax
import jax.numpy as jnp
from jax import lax
from jax.experimental import pallas as pl
from jax.experimental.pallas import tpu as pltpu

D_MODEL = 2048
DEPTH = 2
BLOCK = 128
EPS = 1e-6
A_HEADS = 8
A_HEAD_DIM = 128
A_WIDTH = A_HEADS * A_HEAD_DIM
POOL_WINDOWS = (2, 4, 8, 16)
B_GROUP_DIM = 256
B_WIDTH = len(POOL_WINDOWS) * B_GROUP_DIM
C_PATTERNS = ((128, 1), (512, 4), (2048, 16))
C_GROUPS = len(C_PATTERNS)
C_HEADS_PER_GROUP = 4
C_HEAD_DIM = 128
C_HEADS = C_GROUPS * C_HEADS_PER_GROUP
C_QKV_WIDTH = C_HEADS * C_HEAD_DIM
C_OUT_WIDTH = C_HEADS_PER_GROUP * C_HEAD_DIM
N_BRANCHES = 3
GATE_WIDTH = N_BRANCHES * D_MODEL
IN_WIDTH = 2 * A_WIDTH + B_WIDTH + 3 * C_QKV_WIDTH + GATE_WIDTH
REL_BUCKETS = 32
REL_MAX_DISTANCE = 2048
PEER_HEADS = 8
PEER_KEYS = 128
PEER_N_EXPERTS = PEER_KEYS * PEER_KEYS
PEER_KEY_DIM = 256
PEER_TOPK = 16

COL_GATE = 0
COL_AU = GATE_WIDTH
COL_AV = COL_AU + A_WIDTH
COL_B = COL_AV + A_WIDTH
MAIN_WIDTH = COL_B + B_WIDTH
QKV_GROUP_WIDTH = 3 * C_OUT_WIDTH

NEG_MASK = -1e30
BF16 = jnp.bfloat16
F32 = jnp.float32

VMEM_LIMIT_BYTES = 56 * 1024 * 1024
LANES = 128
PACKED_ROWS = 16
MXU_COLS = 256


def _params(n_axes):
    return pltpu.CompilerParams(
        dimension_semantics=("arbitrary",) * n_axes, vmem_limit_bytes=VMEM_LIMIT_BYTES)


def _rms(x, g):
    return x * lax.rsqrt(jnp.mean(x * x, axis=-1, keepdims=True) + EPS) * g


def _gelu(x):
    return 0.5 * x * (1.0 + lax.erf(x * (1.0 / math.sqrt(2.0))))


def _resident(shape):
    nd = len(shape)
    return pl.BlockSpec(shape, lambda *_: (0,) * nd, pipeline_mode=pl.Buffered(1))


DILATIONS = tuple(d for _, d in C_PATTERNS if d > 1)


def _group_by_residue_matrices(tm):
    mats = []
    for dil in DILATIONS:
        dst = np.arange(tm)
        src = (dst % (tm // dil)) * dil + dst // (tm // dil)
        mats.append((src[:, None] == np.arange(tm)[None, :]).astype(np.float32))
    return np.stack(mats)


def _write_norms(x, g, perm_ref, h_ref, hd_refs):
    h = _rms(x, g).astype(h_ref.dtype)
    h_ref[...] = h
    for k, ref in enumerate(hd_refs):
        grouped = jnp.dot(perm_ref[k], h, preferred_element_type=F32).astype(ref.dtype)
        ref[...] = grouped.reshape(ref.shape)


def _norm_specs(batch, seq, d, tm, index, buffering=None):
    specs = [pl.BlockSpec((tm, d), lambda *ids: (index(*ids)[0] * (seq // tm) + index(*ids)[1], 0),
                          pipeline_mode=buffering)]
    shapes = [jax.ShapeDtypeStruct((batch * seq, d), BF16)]
    for dil in DILATIONS:
        specs.append(pl.BlockSpec((None, dil, tm // dil, d),
                                  lambda *ids: (index(*ids)[0], 0, index(*ids)[1], 0),
                                  pipeline_mode=buffering))
        shapes.append(jax.ShapeDtypeStruct((batch, dil, seq // dil, d), BF16))
    return specs, shapes


def _norm_kernel(x_ref, g_ref, perm_ref, h_ref, *hd_refs):
    _write_norms(x_ref[...], g_ref[...], perm_ref, h_ref, hd_refs)


def rmsnorm_rows(x, g, batch, seq, tm=512):
    t, d = x.shape
    tiles = seq // tm
    specs, shapes = _norm_specs(batch, seq, d, tm, lambda i: (i // tiles, i % tiles))
    perm = jnp.asarray(_group_by_residue_matrices(tm), BF16)
    return pl.pallas_call(
        _norm_kernel,
        grid=(t // tm,),
        in_specs=[pl.BlockSpec((tm, d), lambda i: (i, 0)), _resident((1, d)), _resident(perm.shape)],
        out_specs=specs,
        out_shape=shapes,
        compiler_params=_params(1),
        name="rmsnorm_rows",
    )(x, g.reshape(1, d), perm)


def _matmul_kernel(h_ref, w_ref, o_ref):
    o_ref[...] = jnp.dot(h_ref[...], w_ref[...], preferred_element_type=F32).astype(o_ref.dtype)


def in_proj(h, w, n_out=None, w_block=lambda j: j, tm=1024, tn=QKV_GROUP_WIDTH):
    t, d = h.shape
    n = w.shape[1] if n_out is None else n_out
    return pl.pallas_call(
        _matmul_kernel,
        grid=(n // tn, t // tm),
        in_specs=[pl.BlockSpec((tm, d), lambda j, i: (i, 0)),
                  pl.BlockSpec((d, tn), lambda j, i: (0, w_block(j)))],
        out_specs=pl.BlockSpec((tm, tn), lambda j, i: (i, j)),
        out_shape=jax.ShapeDtypeStruct((t, n), BF16),
        compiler_params=_params(2),
        name="in_proj",
    )(h, w)


def _mixer_a_kernel(au_ref, av_ref, g_ref, ws_ref, bs_ref, o_ref, *, chunks):
    u = _gelu(au_ref[...].astype(F32))
    v = _rms(_gelu(av_ref[...].astype(F32)), g_ref[...]).astype(BF16)
    row = lax.broadcasted_iota(jnp.int32, (BLOCK, BLOCK), 0)
    col = lax.broadcasted_iota(jnp.int32, (BLOCK, BLOCK), 1)
    causal = col <= row
    for hd in range(A_HEADS):
        w = jnp.where(causal, ws_ref[hd], 0.0).astype(BF16)
        bias = bs_ref[hd]
        cs = slice(hd * A_HEAD_DIM, (hd + 1) * A_HEAD_DIM)
        for c in range(chunks):
            rs = slice(c * BLOCK, (c + 1) * BLOCK)
            s = jnp.dot(w, v[rs, cs], preferred_element_type=F32) + bias
            o_ref[rs, cs] = (u[rs, cs] * s).astype(o_ref.dtype)


def mixer_a(p, a_norm_g, a_w_s, a_b_s, chunks=4):
    t = p.shape[0]
    tm = chunks * BLOCK
    bias = jnp.broadcast_to(a_b_s[:, :, None], (A_HEADS, BLOCK, A_HEAD_DIM))
    return pl.pallas_call(
        functools.partial(_mixer_a_kernel, chunks=chunks),
        grid=(t // tm,),
        in_specs=[pl.BlockSpec((tm, A_WIDTH), lambda i: (i, COL_AU // A_WIDTH)),
                  pl.BlockSpec((tm, A_WIDTH), lambda i: (i, COL_AV // A_WIDTH)),
                  _resident((1, A_WIDTH)),
                  _resident((A_HEADS, BLOCK, BLOCK)),
                  _resident((A_HEADS, BLOCK, A_HEAD_DIM))],
        out_specs=pl.BlockSpec((tm, A_WIDTH), lambda i: (i, 0)),
        out_shape=jax.ShapeDtypeStruct((t, A_WIDTH), BF16),
        compiler_params=_params(1),
        name="mixer_a",
    )(p, p, a_norm_g.reshape(1, A_WIDTH), a_w_s, bias)


def _pool_band_matrices():
    t = np.arange(BLOCK)[:, None]
    k = np.arange(2 * BLOCK)[None, :]
    back = BLOCK + t - k
    return np.stack([((back >= 0) & (back < w)) for w in POOL_WINDOWS]).astype(np.float32)


def _mixer_b_kernel(prev_ref, cur_ref, band_ref, wp_ref, sc_ref, o_ref, *, chunks, tiles_per_seq):
    first = (pl.program_id(0) % tiles_per_seq) == 0
    prev = jnp.where(first, jnp.zeros_like(prev_ref[...]), prev_ref[...])
    tile_pos = (pl.program_id(0) % tiles_per_seq) * (chunks * BLOCK)
    for c in range(chunks):
        rs = slice(c * BLOCK, (c + 1) * BLOCK)
        before = prev if c == 0 else cur_ref[(c - 1) * BLOCK:c * BLOCK, :]
        cur = cur_ref[rs, :]
        both = jnp.concatenate([before, cur], axis=0)
        pos = tile_pos + c * BLOCK + lax.broadcasted_iota(jnp.int32, (BLOCK, B_GROUP_DIM), 0)
        for g, w in enumerate(POOL_WINDOWS):
            cs = slice(g * B_GROUP_DIM, (g + 1) * B_GROUP_DIM)
            wsum = jnp.dot(band_ref[g], both[:, cs], preferred_element_type=F32)
            count = jnp.minimum(pos + 1, w).astype(F32)
            diff = (wsum / count - cur[:, cs].astype(F32)).astype(BF16)
            y = jnp.dot(diff, wp_ref[g], preferred_element_type=F32) * sc_ref[:, cs]
            o_ref[rs, cs] = y.astype(o_ref.dtype)


def mixer_b(p, w_pool, scale, seq, chunks=4):
    t = p.shape[0]
    tm = chunks * BLOCK
    band = jnp.asarray(_pool_band_matrices(), BF16)
    col = COL_B // B_WIDTH
    return pl.pallas_call(
        functools.partial(_mixer_b_kernel, chunks=chunks, tiles_per_seq=seq // tm),
        grid=(t // tm,),
        in_specs=[pl.BlockSpec((BLOCK, B_WIDTH), lambda i: (jnp.maximum(i * chunks - 1, 0), col)),
                  pl.BlockSpec((tm, B_WIDTH), lambda i: (i, col)),
                  _resident((len(POOL_WINDOWS), BLOCK, 2 * BLOCK)),
                  _resident((len(POOL_WINDOWS), B_GROUP_DIM, B_GROUP_DIM)),
                  _resident((1, B_WIDTH))],
        out_specs=pl.BlockSpec((tm, B_WIDTH), lambda i: (i, 0)),
        out_shape=jax.ShapeDtypeStruct((t, B_WIDTH), BF16),
        compiler_params=_params(1),
        name="mixer_b",
    )(p, p, band, w_pool, scale.reshape(1, B_WIDTH))


def _rel_bucket(n):
    max_exact = REL_BUCKETS // 2
    n = np.asarray(n)
    nl = np.maximum(n, max_exact).astype(np.float64)
    large = max_exact + (np.log(nl / max_exact) / math.log(REL_MAX_DISTANCE / max_exact)
                         * (REL_BUCKETS - max_exact)).astype(np.int32)
    large = np.minimum(large, REL_BUCKETS - 1)
    return np.where(n < max_exact, n, large).astype(np.int32)


def _attn_bias(rel_table, g, dilation, n_off):
    qi = np.arange(BLOCK)[:, None]
    ki = np.arange(2 * BLOCK)[None, :]
    off = BLOCK + qi - ki
    ok = (off >= 0) & (off <= n_off)
    buckets = _rel_bucket(np.clip(off, 0, n_off) * dilation)
    onehot = (buckets.reshape(1, -1) == np.arange(REL_BUCKETS)[:, None]).astype(np.float32)
    heads = rel_table[:, g * C_HEADS_PER_GROUP:(g + 1) * C_HEADS_PER_GROUP]
    bias = jnp.dot(heads.T.astype(F32), jnp.asarray(onehot), precision=lax.Precision.HIGHEST)
    bias = bias.reshape(C_HEADS_PER_GROUP, BLOCK, 2 * BLOCK)
    return jnp.where(ok[None], bias, NEG_MASK)


ATTN_SUBS_PER_STEP = 4


def _attn_kernel(q_ref, kp_ref, kc_ref, vp_ref, vc_ref, bias_ref, o_ref, lse_ref):
    first = pl.program_id(1) == 0
    key_is_prev = lax.broadcasted_iota(jnp.int32, (BLOCK, 2 * BLOCK), 1) < BLOCK
    hide = jnp.logical_and(first, key_is_prev)
    scale = C_HEAD_DIM ** -0.5
    for r in range(q_ref.shape[0]):
        for hd in range(C_HEADS_PER_GROUP):
            cs = slice(hd * C_HEAD_DIM, (hd + 1) * C_HEAD_DIM)
            keys = jnp.concatenate([kp_ref[r, :, cs], kc_ref[r, :, cs]], axis=0)
            vals = jnp.concatenate([vp_ref[r, :, cs], vc_ref[r, :, cs]], axis=0)
            logits = lax.dot_general(q_ref[r, :, cs], keys, (((1,), (1,)), ((), ())),
                                     preferred_element_type=F32) * scale
            logits = jnp.where(hide, NEG_MASK, logits + bias_ref[hd])
            m = jnp.max(logits, axis=-1, keepdims=True)
            e = jnp.exp(logits - m)
            s = jnp.sum(e, axis=-1, keepdims=True)
            o = jnp.dot(e.astype(BF16), vals, preferred_element_type=F32) / s
            o_ref[r, :, cs] = o.astype(o_ref.dtype)
            lse_ref[r, :, cs] = jnp.broadcast_to(m + jnp.log(s), (BLOCK, C_HEAD_DIM))


def dilated_attention_group(qkv, first_col, rel_table, g, batch, seq):
    window, dilation = C_PATTERNS[g]
    n_off = window // dilation
    sub_len = seq // dilation
    nb = sub_len // BLOCK
    subs = batch * dilation
    pv = qkv.reshape(subs, sub_len, qkv.shape[1])
    bias = _attn_bias(rel_table, g, dilation, n_off)
    c0 = first_col // C_OUT_WIDTH

    def col(k):
        return lambda s, n: (s, n, c0 + k)

    def col_prev(k):
        return lambda s, n: (s, jnp.maximum(n - 1, 0), c0 + k)

    per_step = math.gcd(subs, ATTN_SUBS_PER_STEP)
    blk = (per_step, BLOCK, C_OUT_WIDTH)
    out_blk = pl.BlockSpec(blk, lambda s, n: (s, n, 0))
    o, lse = pl.pallas_call(
        _attn_kernel,
        grid=(subs // per_step, nb),
        in_specs=[pl.BlockSpec(blk, col(0)),
                  pl.BlockSpec(blk, col_prev(1)), pl.BlockSpec(blk, col(1)),
                  pl.BlockSpec(blk, col_prev(2)), pl.BlockSpec(blk, col(2)),
                  _resident((C_HEADS_PER_GROUP, BLOCK, 2 * BLOCK))],
        out_specs=[out_blk, out_blk],
        out_shape=[jax.ShapeDtypeStruct((subs, sub_len, C_OUT_WIDTH), BF16),
                   jax.ShapeDtypeStruct((subs, sub_len, C_OUT_WIDTH), F32)],
        compiler_params=_params(2),
        name=f"dilated_attention_{g}",
    )(pv, pv, pv, pv, pv, bias)
    return o, lse


def _merge_kernel(x_ref, gate_ref, ya_ref, yb_ref, o0_ref, o1_ref, o2_ref, l0_ref, l1_ref, l2_ref,
                  wa_ref, wb_ref, wc_ref, wo_ref, g_ref, perm_ref, xo_ref, h_ref):
    tm = x_ref.shape[0]

    def token_order(ref, k, precision=None):
        return jnp.dot(perm_ref[k].astype(ref.dtype), ref[...].reshape(tm, C_OUT_WIDTH),
                       preferred_element_type=F32, precision=precision)

    exact_f32 = lax.Precision.HIGHEST
    l0, l1, l2 = l0_ref[...], token_order(l1_ref, 0, exact_f32), token_order(l2_ref, 1, exact_f32)
    m = jnp.maximum(jnp.maximum(l0, l1), l2)
    e0, e1, e2 = jnp.exp(l0 - m), jnp.exp(l1 - m), jnp.exp(l2 - m)
    yc = (e0 * o0_ref[...].astype(F32) + e1 * token_order(o1_ref, 0)
          + e2 * token_order(o2_ref, 1)) / (e0 + e1 + e2)

    def branch(k, y, w_ref):
        gate = jax.nn.sigmoid(gate_ref[:, k * D_MODEL:(k + 1) * D_MODEL].astype(F32))
        return gate * jnp.dot(y, w_ref[...], preferred_element_type=F32)

    merged = (branch(0, ya_ref[...], wa_ref) + branch(1, yb_ref[...], wb_ref)
              + branch(2, yc.astype(BF16), wc_ref))
    x = x_ref[...] + jnp.dot(merged.astype(BF16), wo_ref[...], preferred_element_type=F32)
    xo_ref[...] = x
    h_ref[...] = _rms(x, g_ref[...]).T.astype(h_ref.dtype)


def merge_and_project(x, p, ya, yb, outs, lses, wa, wb, wc, wo, g_ffn, batch, seq, tm=256):
    t, d = x.shape
    tiles = seq // tm
    row = lambda w: pl.BlockSpec((tm, w), lambda i: (i, 0))

    def grouped(dil):
        return pl.BlockSpec((None, dil, tm // dil, C_OUT_WIDTH), lambda i: (i // tiles, 0, i % tiles, 0))

    def by_residue(a, dil):
        return a.reshape(batch, dil, seq // dil, C_OUT_WIDTH)

    d1, d2 = DILATIONS
    perm = jnp.asarray(_group_by_residue_matrices(tm).transpose(0, 2, 1), BF16)
    return pl.pallas_call(
        _merge_kernel,
        grid=(t // tm,),
        in_specs=[row(d), row(GATE_WIDTH), row(A_WIDTH), row(B_WIDTH),
                  row(C_OUT_WIDTH), grouped(d1), grouped(d2),
                  row(C_OUT_WIDTH), grouped(d1), grouped(d2),
                  _resident((A_WIDTH, d)), _resident((B_WIDTH, d)), _resident((C_OUT_WIDTH, d)),
                  _resident((d, d)), _resident((1, d)), _resident(perm.shape)],
        out_specs=[row(d), pl.BlockSpec((d, tm), lambda i: (0, i))],
        out_shape=[jax.ShapeDtypeStruct((t, d), F32), jax.ShapeDtypeStruct((d, t), BF16)],
        compiler_params=_params(1),
        name="merge_and_project",
    )(x, p, ya, yb,
      outs[0].reshape(t, C_OUT_WIDTH), by_residue(outs[1], d1), by_residue(outs[2], d2),
      lses[0].reshape(t, C_OUT_WIDTH), by_residue(lses[1], d1), by_residue(lses[2], d2),
      wa, wb, wc, wo, g_ffn.reshape(1, d), perm)


def _candidate_pairs():
    return [(a, b) for a in range(PEER_TOPK) for b in range(PEER_TOPK // (a + 1))]


CAND_ROWS = 64


SUBLANES = 8


def _sorting_network(n):
    pairs, p = [], 1
    while p < n:
        k = p
        while k >= 1:
            for j in range(k % p, n - k, 2 * k):
                for i in range(min(k, n - j - k)):
                    if (i + j) // (2 * p) == (i + j + k) // (2 * p):
                        pairs.append((i + j, i + j + k))
            k //= 2
        p *= 2
    return pairs


def _top16_desc(slabs):
    lst = list(slabs)
    for a, b in _sorting_network(len(lst)):
        lst[a], lst[b] = jnp.maximum(lst[a], lst[b]), jnp.minimum(lst[a], lst[b])
    out = []
    for k in range(PEER_TOPK):
        head = jnp.max(lst[0], axis=0, keepdims=True)
        out.append(head)
        if k == PEER_TOPK - 1:
            break
        taken = lst[0] == head
        keep = min(len(lst), PEER_TOPK - k - 1)
        lst = [jnp.where(taken, lst[a + 1] if a + 1 < len(lst) else -jnp.inf, lst[a])
               for a in range(keep)]
    return out


def _row_slabs(x):
    return [x[r:r + SUBLANES, :] for r in range(0, x.shape[0], SUBLANES)]


COUNT_SPLIT_B = 4
COUNT_SPLIT_A = 3
assert (COUNT_SPLIT_A + 1) * (COUNT_SPLIT_B + 1) > PEER_TOPK


def _peer_select_kernel(h_ref, wq_ref, k1_ref, k2_ref, e2_ref, r2_ref, w_ref, n_ref,
                        cand_ref, v2_ref):
    tm = h_ref.shape[1]
    qt = jnp.dot(wq_ref[...], h_ref[...], preferred_element_type=F32)
    half = PEER_KEY_DIM // 2
    cand_ref[...] = jnp.full(cand_ref.shape, -jnp.inf, F32)
    for hd in range(PEER_HEADS):
        q1 = qt[hd * PEER_KEY_DIM:hd * PEER_KEY_DIM + half, :]
        q2 = qt[hd * PEER_KEY_DIM + half:(hd + 1) * PEER_KEY_DIM, :]
        s1 = jnp.dot(k1_ref[...], q1, preferred_element_type=F32, precision=lax.Precision.HIGHEST)
        s2 = jnp.dot(k2_ref[...], q2, preferred_element_type=F32, precision=lax.Precision.HIGHEST)
        v1 = _top16_desc(_row_slabs(s1))
        v2 = _top16_desc(_row_slabs(s2))
        for b in range(PEER_TOPK):
            v2_ref[b:b + 1, :] = v2[b]
        for idx, (a, b) in enumerate(_candidate_pairs()):
            cand_ref[idx:idx + 1, :] = v1[a] + v2[b]
        best = _top16_desc(_row_slabs(cand_ref[...]))
        tau = best[PEER_TOPK - 1]
        z = functools.reduce(lambda acc, c: acc + jnp.exp(c - best[0]), best[1:],
                             jnp.ones_like(best[0]))
        rank2 = jnp.zeros(s2.shape, F32)
        for b in range(PEER_TOPK):
            rank2 = jnp.where(v2[b] > s2, float(b + 1), rank2)
        count = jnp.zeros(s1.shape, F32)
        for b in range(COUNT_SPLIT_B):
            count = jnp.where(s1 + v2[b] >= tau, float(b + 1), count)
        v2_all = v2_ref[...]
        low_rank = lax.broadcasted_iota(jnp.int32, v2_all.shape, 0) >= COUNT_SPLIT_B
        for a in range(COUNT_SPLIT_A):
            reached = jnp.logical_and(v1[a] + v2_all >= tau, low_rank)
            extra = jnp.sum(jnp.where(reached, 1.0, 0.0), axis=0, keepdims=True)
            count = count + jnp.where(s1 == v1[a], extra, 0.0)
        e2_ref[hd] = jnp.exp(s2 - v2[0]).astype(e2_ref.dtype)
        r2_ref[hd] = rank2.astype(r2_ref.dtype)
        w_ref[hd] = jnp.exp(s1 - v1[0]) / z
        n_ref[hd] = count


def peer_select(h_t, wq_t, k1, k2, tm=512):
    d, t = h_t.shape
    tab = lambda dt: jax.ShapeDtypeStruct((PEER_HEADS, PEER_KEYS, t), dt)
    blk = pl.BlockSpec((PEER_HEADS, PEER_KEYS, tm), lambda i: (0, 0, i))
    return pl.pallas_call(
        _peer_select_kernel,
        grid=(t // tm,),
        in_specs=[pl.BlockSpec((d, tm), lambda i: (0, i)),
                  _resident((PEER_HEADS * PEER_KEY_DIM, d)),
                  _resident((PEER_KEYS, PEER_KEY_DIM // 2)),
                  _resident((PEER_KEYS, PEER_KEY_DIM // 2))],
        out_specs=[blk, blk, blk, blk],
        out_shape=[tab(BF16), tab(BF16), tab(F32), tab(F32)],
        scratch_shapes=[pltpu.VMEM((CAND_ROWS, tm), F32), pltpu.VMEM((PEER_TOPK, tm), F32)],
        compiler_params=_params(1),
        name="peer_select",
    )(h_t, wq_t, k1, k2)


def _peer_dense_kernel(h_ref, u_ref, vt_ref, e2_ref, r2_ref, w_ref, n_ref, x_ref, g_ref,
                       *rest, first_keys, final):
    if final:
        out_ref, acc_ref = rest
    else:
        perm_ref, xo_ref, hn_ref, *hd_refs, acc_ref = rest
    e = pl.program_id(1)
    tm = h_ref.shape[1]

    @pl.when(e == 0)
    def _():
        acc_ref[...] = jnp.zeros_like(acc_ref)

    first_key = e * first_keys
    gates = []
    for ii in range(first_keys):
        g = None
        for hd in range(PEER_HEADS):
            n_row = jnp.broadcast_to(n_ref[hd, pl.ds(first_key + ii, 1), :].astype(BF16), (PEER_KEYS, tm))
            w_row = jnp.broadcast_to(w_ref[hd, pl.ds(first_key + ii, 1), :].astype(BF16), (PEER_KEYS, tm))
            picked = jnp.where(r2_ref[hd] < n_row, e2_ref[hd], jnp.zeros((), BF16)) * w_row
            g = picked if g is None else g + picked
        gates.append(g)
    gate_t = jnp.concatenate(gates, axis=0)

    a_t = jnp.dot(u_ref[...], h_ref[...], preferred_element_type=F32)
    act_t = _gelu(a_t).astype(BF16) * gate_t
    acc_ref[...] += jnp.dot(vt_ref[...], act_t, preferred_element_type=F32)

    @pl.when(e == pl.num_programs(1) - 1)
    def _():
        x = x_ref[...] + acc_ref[...].T
        if final:
            out_ref[...] = _rms(x, g_ref[...])
        else:
            xo_ref[...] = x
            _write_norms(x, g_ref[...], perm_ref, hn_ref, hd_refs)


def peer_dense(h_t, u, v, e2, r2, w, n, x, g_next, final, batch, seq, tm=512, te=1024):
    d, t = h_t.shape
    tiles = seq // tm
    first_keys = te // PEER_KEYS
    n_tiles = PEER_N_EXPERTS // te
    tab = pl.BlockSpec((PEER_HEADS, PEER_KEYS, tm), lambda i, e: (0, 0, i), pipeline_mode=pl.Buffered(1))
    tok = pl.BlockSpec((tm, d), lambda i, e: (i, 0))
    v_tiles = v.reshape(n_tiles, te, d).transpose(0, 2, 1).astype(BF16)
    operands = [h_t, u, v_tiles, e2, r2, w, n, x, g_next.reshape(1, d)]
    in_specs = [pl.BlockSpec((d, tm), lambda i, e: (0, i)),
                pl.BlockSpec((te, d), lambda i, e: (e, 0)),
                pl.BlockSpec((None, d, te), lambda i, e: (e, 0, 0)),
                tab, tab, tab, tab,
                pl.BlockSpec((tm, d), lambda i, e: (i, 0), pipeline_mode=pl.Buffered(1)),
                _resident((1, d))]
    once = pl.Buffered(1)
    tok_out = pl.BlockSpec((tm, d), lambda i, e: (i, 0), pipeline_mode=once)
    if final:
        out_specs = tok_out
        out_shape = jax.ShapeDtypeStruct((t, d), F32)
    else:
        specs, shapes = _norm_specs(batch, seq, d, tm, lambda i, e: (i // tiles, i % tiles), once)
        out_specs = [tok_out] + specs
        out_shape = [jax.ShapeDtypeStruct((t, d), F32)] + shapes
        perm = jnp.asarray(_group_by_residue_matrices(tm), BF16)
        operands.append(perm)
        in_specs.append(_resident(perm.shape))
    return pl.pallas_call(
        functools.partial(_peer_dense_kernel, first_keys=first_keys, final=final),
        grid=(t // tm, n_tiles),
        in_specs=in_specs,
        out_specs=out_specs,
        out_shape=out_shape,
        scratch_shapes=[pltpu.VMEM((d, tm), F32)],
        compiler_params=_params(2),
        name="peer_dense_final" if final else "peer_dense",
    )(*operands)


W_IN_GATE_COL = IN_WIDTH - GATE_WIDTH
W_IN_Q_COL = 2 * A_WIDTH + B_WIDTH


def _main_weight_block(j):
    gate_blocks = GATE_WIDTH // QKV_GROUP_WIDTH
    return jnp.where(j < gate_blocks, j + W_IN_GATE_COL // QKV_GROUP_WIDTH, j - gate_blocks)


def _qkv_weights(w, g):
    cols = [W_IN_Q_COL + part * C_QKV_WIDTH + g * C_OUT_WIDTH for part in range(3)]
    return jnp.concatenate([w[:, c:c + C_OUT_WIDTH] for c in cols], axis=1).astype(BF16)


def kernel(x, rel_bias_table, norm_mix_g, w_in, a_norm_g, a_w_s, a_b_s, b_w_pool, b_scale, w_br_a,
           w_br_b, w_br_c, w_out, norm_ffn_g, peer_w_q, peer_sub_keys_1, peer_sub_keys_2, peer_u,
           peer_v, final_norm_g):
    batch, seq, d = x.shape
    t = batch * seq
    xf = x.reshape(t, d)
    hs = rmsnorm_rows(xf, norm_mix_g[0], batch, seq)
    out = None
    for l in range(DEPTH):
        p = in_proj(hs[0], w_in[l].astype(BF16), MAIN_WIDTH, _main_weight_block)
        ya = mixer_a(p, a_norm_g[l], a_w_s[l], a_b_s[l])
        yb = mixer_b(p, b_w_pool[l].astype(BF16), b_scale[l], seq)
        outs, lses = [], []
        for g in range(C_GROUPS):
            qkv = in_proj(hs[g].reshape(t, d), _qkv_weights(w_in[l], g))
            o, lse = dilated_attention_group(qkv, 0, rel_bias_table, g, batch, seq)
            outs.append(o)
            lses.append(lse)
        x_mid, h_ffn = merge_and_project(
            xf, p, ya, yb, outs, lses, w_br_a[l].astype(BF16), w_br_b[l].astype(BF16),
            w_br_c[l].astype(BF16), w_out[l].astype(BF16), norm_ffn_g[l], batch, seq)
        e2, r2, w, n = peer_select(h_ffn, peer_w_q[l].T.astype(BF16), peer_sub_keys_1[l],
                                   peer_sub_keys_2[l])
        final = l == DEPTH - 1
        g_next = final_norm_g if final else norm_mix_g[l + 1]
        res = peer_dense(h_ffn, peer_u[l].astype(BF16), peer_v[l], e2, r2, w, n,
                         x_mid, g_next, final, batch, seq)
        if final:
            out = res
        else:
            xf, hs = res[0], res[1:]
    return out.reshape(batch, seq, d)
```

```python
import functools
import math

import numpy as np
import jax
import jax.numpy as jnp
from jax import lax
from jax.experimental import pallas as pl
from jax.experimental.pallas import tpu as pltpu

D_MODEL = 2048
DEPTH = 2
BLOCK = 128
EPS = 1e-6
A_HEADS = 8
A_HEAD_DIM = 128
A_WIDTH = A_HEADS * A_HEAD_DIM
POOL_WINDOWS = (2, 4, 8, 16)
B_GROUP_DIM = 256
B_WIDTH = len(POOL_WINDOWS) * B_GROUP_DIM
C_PATTERNS = ((128, 1), (512, 4), (2048, 16))
C_GROUPS = len(C_PATTERNS)
C_HEADS_PER_GROUP = 4
C_HEAD_DIM = 128
C_HEADS = C_GROUPS * C_HEADS_PER_GROUP
C_QKV_WIDTH = C_HEADS * C_HEAD_DIM
C_OUT_WIDTH = C_HEADS_PER_GROUP * C_HEAD_DIM
N_BRANCHES = 3
GATE_WIDTH = N_BRANCHES * D_MODEL
IN_WIDTH = 2 * A_WIDTH + B_WIDTH + 3 * C_QKV_WIDTH + GATE_WIDTH
REL_BUCKETS = 32
REL_MAX_DISTANCE = 2048
PEER_HEADS = 8
PEER_KEYS = 128
PEER_N_EXPERTS = PEER_KEYS * PEER_KEYS
PEER_KEY_DIM = 256
PEER_TOPK = 16

COL_GATE = 0
COL_AU = GATE_WIDTH
COL_AV = COL_AU + A_WIDTH
COL_B = COL_AV + A_WIDTH
MAIN_WIDTH = COL_B + B_WIDTH
QKV_GROUP_WIDTH = 3 * C_OUT_WIDTH

NEG_MASK = -1e30
BF16 = jnp.bfloat16
F32 = jnp.float32

VMEM_LIMIT_BYTES = 56 * 1024 * 1024
LANES = 128
PACKED_ROWS = 16
MXU_COLS = 256


def _params(n_axes):
    return pltpu.CompilerParams(
        dimension_semantics=("arbitrary",) * n_axes, vmem_limit_bytes=VMEM_LIMIT_BYTES)


def _rms(x, g):
    return x * lax.rsqrt(jnp.mean(x * x, axis=-1, keepdims=True) + EPS) * g


def _gelu(x):
    return 0.5 * x * (1.0 + lax.erf(x * (1.0 / math.sqrt(2.0))))


def _resident(shape):
    nd = len(shape)
    return pl.BlockSpec(shape, lambda *_: (0,) * nd, pipeline_mode=pl.Buffered(1))


DILATIONS = tuple(d for _, d in C_PATTERNS if d > 1)


def _group_by_residue_matrices(tm):
    mats = []
    for dil in DILATIONS:
        dst = np.arange(tm)
        src = (dst % (tm // dil)) * dil + dst // (tm // dil)
        mats.append((src[:, None] == np.arange(tm)[None, :]).astype(np.float32))
    return np.stack(mats)


def _write_norms(x, g, perm_ref, h_ref, hd_refs):
    h = _rms(x, g).astype(h_ref.dtype)
    h_ref[...] = h
    for k, ref in enumerate(hd_refs):
        grouped = jnp.dot(perm_ref[k], h, preferred_element_type=F32).astype(ref.dtype)
        ref[...] = grouped.reshape(ref.shape)


def _norm_specs(batch, seq, d, tm, index, buffering=None):
    specs = [pl.BlockSpec((tm, d), lambda *ids: (index(*ids)[0] * (seq // tm) + index(*ids)[1], 0),
                          pipeline_mode=buffering)]
    shapes = [jax.ShapeDtypeStruct((batch * seq, d), BF16)]
    for dil in DILATIONS:
        specs.append(pl.BlockSpec((None, dil, tm // dil, d),
                                  lambda *ids: (index(*ids)[0], 0, index(*ids)[1], 0),
                                  pipeline_mode=buffering))
        shapes.append(jax.ShapeDtypeStruct((batch, dil, seq // dil, d), BF16))
    return specs, shapes


def _norm_kernel(x_ref, g_ref, perm_ref, h_ref, *hd_refs):
    _write_norms(x_ref[...], g_ref[...], perm_ref, h_ref, hd_refs)


def rmsnorm_rows(x, g, batch, seq, tm=512):
    t, d = x.shape
    tiles = seq // tm
    specs, shapes = _norm_specs(batch, seq, d, tm, lambda i: (i // tiles, i % tiles))
    perm = jnp.asarray(_group_by_residue_matrices(tm), BF16)
    return pl.pallas_call(
        _norm_kernel,
        grid=(t // tm,),
        in_specs=[pl.BlockSpec((tm, d), lambda i: (i, 0)), _resident((1, d)), _resident(perm.shape)],
        out_specs=specs,
        out_shape=shapes,
        compiler_params=_params(1),
        name="rmsnorm_rows",
    )(x, g.reshape(1, d), perm)


def _cast_kernel(x_ref, o_ref):
    o_ref[...] = x_ref[...].astype(o_ref.dtype)


def to_bf16(x, rows):
    n, d = x.shape
    return pl.pallas_call(
        _cast_kernel,
        grid=(n // rows,),
        in_specs=[pl.BlockSpec((rows, d), lambda i: (i, 0))],
        out_specs=pl.BlockSpec((rows, d), lambda i: (i, 0)),
        out_shape=jax.ShapeDtypeStruct((n, d), BF16),
        compiler_params=_params(1),
        name="to_bf16",
    )(x)


def _transpose_cast_kernel(x_ref, o_ref):
    o_ref[...] = x_ref[...].T.astype(o_ref.dtype)


def transposed_tiles_bf16(x, rows):
    n, d = x.shape
    return pl.pallas_call(
        _transpose_cast_kernel,
        grid=(n // rows,),
        in_specs=[pl.BlockSpec((rows, d), lambda i: (i, 0))],
        out_specs=pl.BlockSpec((None, d, rows), lambda i: (i, 0, 0)),
        out_shape=jax.ShapeDtypeStruct((n // rows, d, rows), BF16),
        compiler_params=_params(1),
        name="transposed_tiles_bf16",
    )(x)


def _matmul_kernel(h_ref, w_ref, o_ref):
    o_ref[...] = jnp.dot(h_ref[...], w_ref[...], preferred_element_type=F32).astype(o_ref.dtype)


def in_proj(h, w, n_out=None, w_block=lambda j: j, tm=1024, tn=QKV_GROUP_WIDTH):
    t, d = h.shape
    n = w.shape[1] if n_out is None else n_out
    return pl.pallas_call(
        _matmul_kernel,
        grid=(n // tn, t // tm),
        in_specs=[pl.BlockSpec((tm, d), lambda j, i: (i, 0)),
                  pl.BlockSpec((d, tn), lambda j, i: (0, w_block(j)))],
        out_specs=pl.BlockSpec((tm, tn), lambda j, i: (i, j)),
        out_shape=jax.ShapeDtypeStruct((t, n), BF16),
        compiler_params=_params(2),
        name="in_proj",
    )(h, w)


def _mixer_a_kernel(au_ref, av_ref, g_ref, ws_ref, bs_ref, o_ref, *, chunks):
    u = _gelu(au_ref[...].astype(F32))
    v = _rms(_gelu(av_ref[...].astype(F32)), g_ref[...]).astype(BF16)
    row = lax.broadcasted_iota(jnp.int32, (BLOCK, BLOCK), 0)
    col = lax.broadcasted_iota(jnp.int32, (BLOCK, BLOCK), 1)
    causal = col <= row
    for hd in range(A_HEADS):
        w = jnp.where(causal, ws_ref[hd], 0.0).astype(BF16)
        bias = bs_ref[hd]
        cs = slice(hd * A_HEAD_DIM, (hd + 1) * A_HEAD_DIM)
        for c in range(chunks):
            rs = slice(c * BLOCK, (c + 1) * BLOCK)
            s = jnp.dot(w, v[rs, cs], preferred_element_type=F32) + bias
            o_ref[rs, cs] = (u[rs, cs] * s).astype(o_ref.dtype)


def mixer_a(p, a_norm_g, a_w_s, a_b_s, chunks=4):
    t = p.shape[0]
    tm = chunks * BLOCK
    bias = jnp.broadcast_to(a_b_s[:, :, None], (A_HEADS, BLOCK, A_HEAD_DIM))
    return pl.pallas_call(
        functools.partial(_mixer_a_kernel, chunks=chunks),
        grid=(t // tm,),
        in_specs=[pl.BlockSpec((tm, A_WIDTH), lambda i: (i, COL_AU // A_WIDTH)),
                  pl.BlockSpec((tm, A_WIDTH), lambda i: (i, COL_AV // A_WIDTH)),
                  _resident((1, A_WIDTH)),
                  _resident((A_HEADS, BLOCK, BLOCK)),
                  _resident((A_HEADS, BLOCK, A_HEAD_DIM))],
        out_specs=pl.BlockSpec((tm, A_WIDTH), lambda i: (i, 0)),
        out_shape=jax.ShapeDtypeStruct((t, A_WIDTH), BF16),
        compiler_params=_params(1),
        name="mixer_a",
    )(p, p, a_norm_g.reshape(1, A_WIDTH), a_w_s, bias)


def _pool_band_matrices():
    t = np.arange(BLOCK)[:, None]
    k = np.arange(2 * BLOCK)[None, :]
    back = BLOCK + t - k
    return np.stack([((back >= 0) & (back < w)) for w in POOL_WINDOWS]).astype(np.float32)


def _mixer_b_kernel(prev_ref, cur_ref, band_ref, wp_ref, sc_ref, o_ref, *, chunks, tiles_per_seq):
    first = (pl.program_id(0) % tiles_per_seq) == 0
    prev = jnp.where(first, jnp.zeros_like(prev_ref[...]), prev_ref[...])
    tile_pos = (pl.program_id(0) % tiles_per_seq) * (chunks * BLOCK)
    for c in range(chunks):
        rs = slice(c * BLOCK, (c + 1) * BLOCK)
        before = prev if c == 0 else cur_ref[(c - 1) * BLOCK:c * BLOCK, :]
        cur = cur_ref[rs, :]
        both = jnp.concatenate([before, cur], axis=0)
        pos = tile_pos + c * BLOCK + lax.broadcasted_iota(jnp.int32, (BLOCK, B_GROUP_DIM), 0)
        for g, w in enumerate(POOL_WINDOWS):
            cs = slice(g * B_GROUP_DIM, (g + 1) * B_GROUP_DIM)
            wsum = jnp.dot(band_ref[g], both[:, cs], preferred_element_type=F32)
            count = jnp.minimum(pos + 1, w).astype(F32)
            diff = (wsum / count - cur[:, cs].astype(F32)).astype(BF16)
            y = jnp.dot(diff, wp_ref[g], preferred_element_type=F32) * sc_ref[:, cs]
            o_ref[rs, cs] = y.astype(o_ref.dtype)


def mixer_b(p, w_pool, scale, seq, chunks=4):
    t = p.shape[0]
    tm = chunks * BLOCK
    band = jnp.asarray(_pool_band_matrices(), BF16)
    col = COL_B // B_WIDTH
    return pl.pallas_call(
        functools.partial(_mixer_b_kernel, chunks=chunks, tiles_per_seq=seq // tm),
        grid=(t // tm,),
        in_specs=[pl.BlockSpec((BLOCK, B_WIDTH), lambda i: (jnp.maximum(i * chunks - 1, 0), col)),
                  pl.BlockSpec((tm, B_WIDTH), lambda i: (i, col)),
                  _resident((len(POOL_WINDOWS), BLOCK, 2 * BLOCK)),
                  _resident((len(POOL_WINDOWS), B_GROUP_DIM, B_GROUP_DIM)),
                  _resident((1, B_WIDTH))],
        out_specs=pl.BlockSpec((tm, B_WIDTH), lambda i: (i, 0)),
        out_shape=jax.ShapeDtypeStruct((t, B_WIDTH), BF16),
        compiler_params=_params(1),
        name="mixer_b",
    )(p, p, band, w_pool, scale.reshape(1, B_WIDTH))


def _rel_bucket(n):
    max_exact = REL_BUCKETS // 2
    n = np.asarray(n)
    nl = np.maximum(n, max_exact).astype(np.float64)
    large = max_exact + (np.log(nl / max_exact) / math.log(REL_MAX_DISTANCE / max_exact)
                         * (REL_BUCKETS - max_exact)).astype(np.int32)
    large = np.minimum(large, REL_BUCKETS - 1)
    return np.where(n < max_exact, n, large).astype(np.int32)


def _attn_bias(rel_table, g, dilation, n_off):
    qi = np.arange(BLOCK)[:, None]
    ki = np.arange(2 * BLOCK)[None, :]
    off = BLOCK + qi - ki
    ok = (off >= 0) & (off <= n_off)
    buckets = _rel_bucket(np.clip(off, 0, n_off) * dilation)
    onehot = (buckets.reshape(1, -1) == np.arange(REL_BUCKETS)[:, None]).astype(np.float32)
    heads = rel_table[:, g * C_HEADS_PER_GROUP:(g + 1) * C_HEADS_PER_GROUP]
    bias = jnp.dot(heads.T.astype(F32), jnp.asarray(onehot), precision=lax.Precision.HIGHEST)
    bias = bias.reshape(C_HEADS_PER_GROUP, BLOCK, 2 * BLOCK)
    return jnp.where(ok[None], bias, NEG_MASK)


ATTN_SUBS_PER_STEP = 4


def _attn_kernel(q_ref, kp_ref, kc_ref, vp_ref, vc_ref, bias_ref, o_ref, lse_ref):
    first = pl.program_id(1) == 0
    key_is_prev = lax.broadcasted_iota(jnp.int32, (BLOCK, 2 * BLOCK), 1) < BLOCK
    hide = jnp.logical_and(first, key_is_prev)
    scale = C_HEAD_DIM ** -0.5
    for r in range(q_ref.shape[0]):
        for hd in range(C_HEADS_PER_GROUP):
            cs = slice(hd * C_HEAD_DIM, (hd + 1) * C_HEAD_DIM)
            keys = jnp.concatenate([kp_ref[r, :, cs], kc_ref[r, :, cs]], axis=0)
            vals = jnp.concatenate([vp_ref[r, :, cs], vc_ref[r, :, cs]], axis=0)
            logits = lax.dot_general(q_ref[r, :, cs], keys, (((1,), (1,)), ((), ())),
                                     preferred_element_type=F32) * scale
            logits = jnp.where(hide, NEG_MASK, logits + bias_ref[hd])
            m = jnp.max(logits, axis=-1, keepdims=True)
            e = jnp.exp(logits - m)
            s = jnp.sum(e, axis=-1, keepdims=True)
            o = jnp.dot(e.astype(BF16), vals, preferred_element_type=F32) / s
            o_ref[r, :, cs] = o.astype(o_ref.dtype)
            lse_ref[r, :, cs] = jnp.broadcast_to(m + jnp.log(s), (BLOCK, C_HEAD_DIM))


def dilated_attention_group(qkv, first_col, rel_table, g, batch, seq):
    window, dilation = C_PATTERNS[g]
    n_off = window // dilation
    sub_len = seq // dilation
    nb = sub_len // BLOCK
    subs = batch * dilation
    pv = qkv.reshape(subs, sub_len, qkv.shape[1])
    bias = _attn_bias(rel_table, g, dilation, n_off)
    c0 = first_col // C_OUT_WIDTH

    def col(k):
        return lambda s, n: (s, n, c0 + k)

    def col_prev(k):
        return lambda s, n: (s, jnp.maximum(n - 1, 0), c0 + k)

    per_step = math.gcd(subs, ATTN_SUBS_PER_STEP)
    blk = (per_step, BLOCK, C_OUT_WIDTH)
    out_blk = pl.BlockSpec(blk, lambda s, n: (s, n, 0))
    o, lse = pl.pallas_call(
        _attn_kernel,
        grid=(subs // per_step, nb),
        in_specs=[pl.BlockSpec(blk, col(0)),
                  pl.BlockSpec(blk, col_prev(1)), pl.BlockSpec(blk, col(1)),
                  pl.BlockSpec(blk, col_prev(2)), pl.BlockSpec(blk, col(2)),
                  _resident((C_HEADS_PER_GROUP, BLOCK, 2 * BLOCK))],
        out_specs=[out_blk, out_blk],
        out_shape=[jax.ShapeDtypeStruct((subs, sub_len, C_OUT_WIDTH), BF16),
                   jax.ShapeDtypeStruct((subs, sub_len, C_OUT_WIDTH), F32)],
        compiler_params=_params(2),
        name=f"dilated_attention_{g}",
    )(pv, pv, pv, pv, pv, bias)
    return o, lse


def _merge_kernel(x_ref, gate_ref, ya_ref, yb_ref, o0_ref, o1_ref, o2_ref, l0_ref, l1_ref, l2_ref,
                  wa_ref, wb_ref, wc_ref, wo_ref, g_ref, perm_ref, xo_ref, h_ref):
    tm = x_ref.shape[0]

    def token_order(ref, k, precision=None):
        return jnp.dot(perm_ref[k].astype(ref.dtype), ref[...].reshape(tm, C_OUT_WIDTH),
                       preferred_element_type=F32, precision=precision)

    exact_f32 = lax.Precision.HIGHEST
    l0, l1, l2 = l0_ref[...], token_order(l1_ref, 0, exact_f32), token_order(l2_ref, 1, exact_f32)
    m = jnp.maximum(jnp.maximum(l0, l1), l2)
    e0, e1, e2 = jnp.exp(l0 - m), jnp.exp(l1 - m), jnp.exp(l2 - m)
    yc = (e0 * o0_ref[...].astype(F32) + e1 * token_order(o1_ref, 0)
          + e2 * token_order(o2_ref, 1)) / (e0 + e1 + e2)

    def branch(k, y, w_ref):
        gate = jax.nn.sigmoid(gate_ref[:, k * D_MODEL:(k + 1) * D_MODEL].astype(F32))
        return gate * jnp.dot(y, w_ref[...], preferred_element_type=F32)

    merged = (branch(0, ya_ref[...], wa_ref) + branch(1, yb_ref[...], wb_ref)
              + branch(2, yc.astype(BF16), wc_ref))
    x = x_ref[...] + jnp.dot(merged.astype(BF16), wo_ref[...], preferred_element_type=F32)
    xo_ref[...] = x
    h_ref[...] = _rms(x, g_ref[...]).T.astype(h_ref.dtype)


def merge_and_project(x, p, ya, yb, outs, lses, wa, wb, wc, wo, g_ffn, batch, seq, tm=256):
    t, d = x.shape
    tiles = seq // tm
    row = lambda w: pl.BlockSpec((tm, w), lambda i: (i, 0))

    def grouped(dil):
        return pl.BlockSpec((None, dil, tm // dil, C_OUT_WIDTH), lambda i: (i // tiles, 0, i % tiles, 0))

    def by_residue(a, dil):
        return a.reshape(batch, dil, seq // dil, C_OUT_WIDTH)

    d1, d2 = DILATIONS
    perm = jnp.asarray(_group_by_residue_matrices(tm).transpose(0, 2, 1), BF16)
    return pl.pallas_call(
        _merge_kernel,
        grid=(t // tm,),
        in_specs=[row(d), row(GATE_WIDTH), row(A_WIDTH), row(B_WIDTH),
                  row(C_OUT_WIDTH), grouped(d1), grouped(d2),
                  row(C_OUT_WIDTH), grouped(d1), grouped(d2),
                  _resident((A_WIDTH, d)), _resident((B_WIDTH, d)), _resident((C_OUT_WIDTH, d)),
                  _resident((d, d)), _resident((1, d)), _resident(perm.shape)],
        out_specs=[row(d), pl.BlockSpec((d, tm), lambda i: (0, i))],
        out_shape=[jax.ShapeDtypeStruct((t, d), F32), jax.ShapeDtypeStruct((d, t), BF16)],
        compiler_params=_params(1),
        name="merge_and_project",
    )(x, p, ya, yb,
      outs[0].reshape(t, C_OUT_WIDTH), by_residue(outs[1], d1), by_residue(outs[2], d2),
      lses[0].reshape(t, C_OUT_WIDTH), by_residue(lses[1], d1), by_residue(lses[2], d2),
      wa, wb, wc, wo, g_ffn.reshape(1, d), perm)


def _candidate_pairs():
    return [(a, b) for a in range(PEER_TOPK) for b in range(PEER_TOPK // (a + 1))]


CAND_ROWS = 64


SUBLANES = 8


def _sorting_network(n):
    pairs, p = [], 1
    while p < n:
        k = p
        while k >= 1:
            for j in range(k % p, n - k, 2 * k):
                for i in range(min(k, n - j - k)):
                    if (i + j) // (2 * p) == (i + j + k) // (2 * p):
                        pairs.append((i + j, i + j + k))
            k //= 2
        p *= 2
    return pairs


def _top16_desc(slabs):
    lst = list(slabs)
    for a, b in _sorting_network(len(lst)):
        lst[a], lst[b] = jnp.maximum(lst[a], lst[b]), jnp.minimum(lst[a], lst[b])
    out = []
    for k in range(PEER_TOPK):
        head = jnp.max(lst[0], axis=0, keepdims=True)
        out.append(head)
        if k == PEER_TOPK - 1:
            break
        taken = lst[0] == head
        keep = min(len(lst), PEER_TOPK - k - 1)
        lst = [jnp.where(taken, lst[a + 1] if a + 1 < len(lst) else -jnp.inf, lst[a])
               for a in range(keep)]
    return out


def _row_slabs(x):
    return [x[r:r + SUBLANES, :] for r in range(0, x.shape[0], SUBLANES)]


COUNT_SPLIT_B = 4
COUNT_SPLIT_A = 3
assert (COUNT_SPLIT_A + 1) * (COUNT_SPLIT_B + 1) > PEER_TOPK


def _fold_keys_kernel(k_ref, wq_ref, o_ref):
    o_ref[...] = jnp.dot(k_ref[...], wq_ref[...], preferred_element_type=F32,
                         precision=lax.Precision.HIGHEST).astype(o_ref.dtype)


def fold_sub_keys(wq_t, k1, k2):
    n, d = wq_t.shape
    half = PEER_KEY_DIM // 2
    keys = jnp.stack([k1, k2])
    return pl.pallas_call(
        _fold_keys_kernel,
        grid=(n // half,),
        in_specs=[pl.BlockSpec((None, PEER_KEYS, half), lambda i: (i % 2, 0, 0)),
                  pl.BlockSpec((half, d), lambda i: (i, 0))],
        out_specs=pl.BlockSpec((PEER_KEYS, d), lambda i: (i, 0)),
        out_shape=jax.ShapeDtypeStruct((n // half * PEER_KEYS, d), BF16),
        compiler_params=_params(1),
        name="fold_sub_keys",
    )(keys, wq_t)


def _peer_select_kernel(h_ref, ws_ref, e2_ref, r2_ref, w_ref, n_ref, cand_ref, v2_ref):
    tm = h_ref.shape[1]
    scores = jnp.dot(ws_ref[...], h_ref[...], preferred_element_type=F32)
    cand_ref[...] = jnp.full(cand_ref.shape, -jnp.inf, F32)
    for hd in range(PEER_HEADS):
        s1 = scores[(2 * hd) * PEER_KEYS:(2 * hd + 1) * PEER_KEYS, :]
        s2 = scores[(2 * hd + 1) * PEER_KEYS:(2 * hd + 2) * PEER_KEYS, :]
        v1 = _top16_desc(_row_slabs(s1))
        v2 = _top16_desc(_row_slabs(s2))
        for b in range(PEER_TOPK):
            v2_ref[b:b + 1, :] = v2[b]
        for idx, (a, b) in enumerate(_candidate_pairs()):
            cand_ref[idx:idx + 1, :] = v1[a] + v2[b]
        best = _top16_desc(_row_slabs(cand_ref[...]))
        tau = best[PEER_TOPK - 1]
        z = functools.reduce(lambda acc, c: acc + jnp.exp(c - best[0]), best[1:],
                             jnp.ones_like(best[0]))
        rank2 = jnp.zeros(s2.shape, F32)
        for b in range(PEER_TOPK):
            rank2 = jnp.where(v2[b] > s2, float(b + 1), rank2)
        count = jnp.zeros(s1.shape, F32)
        for b in range(COUNT_SPLIT_B):
            count = jnp.where(s1 + v2[b] >= tau, float(b + 1), count)
        v2_all = v2_ref[...]
        low_rank = lax.broadcasted_iota(jnp.int32, v2_all.shape, 0) >= COUNT_SPLIT_B
        for a in range(COUNT_SPLIT_A):
            reached = jnp.logical_and(v1[a] + v2_all >= tau, low_rank)
            extra = jnp.sum(jnp.where(reached, 1.0, 0.0), axis=0, keepdims=True)
            count = count + jnp.where(s1 == v1[a], extra, 0.0)
        e2_ref[hd] = jnp.exp(s2 - v2[0]).astype(e2_ref.dtype)
        r2_ref[hd] = rank2.astype(r2_ref.dtype)
        w_ref[hd] = jnp.exp(s1 - v1[0]) / z
        n_ref[hd] = count


def peer_select(h_t, w_scores, tm=512):
    d, t = h_t.shape
    tab = lambda dt: jax.ShapeDtypeStruct((PEER_HEADS, PEER_KEYS, t), dt)
    blk = pl.BlockSpec((PEER_HEADS, PEER_KEYS, tm), lambda i: (0, 0, i))
    return pl.pallas_call(
        _peer_select_kernel,
        grid=(t // tm,),
        in_specs=[pl.BlockSpec((d, tm), lambda i: (0, i)),
                  _resident(w_scores.shape)],
        out_specs=[blk, blk, blk, blk],
        out_shape=[tab(BF16), tab(BF16), tab(F32), tab(F32)],
        scratch_shapes=[pltpu.VMEM((CAND_ROWS, tm), F32), pltpu.VMEM((PEER_TOPK, tm), F32)],
        compiler_params=_params(1),
        name="peer_select",
    )(h_t, w_scores)


def _peer_dense_kernel(h_ref, u_ref, vt_ref, e2_ref, r2_ref, w_ref, n_ref, x_ref, g_ref,
                       *rest, first_keys, final):
    if final:
        out_ref, acc_ref = rest
    else:
        perm_ref, xo_ref, hn_ref, *hd_refs, acc_ref = rest
    e = pl.program_id(1)
    tm = h_ref.shape[1]

    @pl.when(e == 0)
    def _():
        acc_ref[...] = jnp.zeros_like(acc_ref)

    first_key = e * first_keys
    gates = []
    for ii in range(first_keys):
        g = None
        for hd in range(PEER_HEADS):
            n_row = jnp.broadcast_to(n_ref[hd, pl.ds(first_key + ii, 1), :].astype(BF16), (PEER_KEYS, tm))
            w_row = jnp.broadcast_to(w_ref[hd, pl.ds(first_key + ii, 1), :].astype(BF16), (PEER_KEYS, tm))
            picked = jnp.where(r2_ref[hd] < n_row, e2_ref[hd], jnp.zeros((), BF16)) * w_row
            g = picked if g is None else g + picked
        gates.append(g)
    gate_t = jnp.concatenate(gates, axis=0)

    a_t = jnp.dot(u_ref[...], h_ref[...], preferred_element_type=F32)
    act_t = _gelu(a_t).astype(BF16) * gate_t
    acc_ref[...] += jnp.dot(vt_ref[...], act_t, preferred_element_type=F32)

    @pl.when(e == pl.num_programs(1) - 1)
    def _():
        x = x_ref[...] + acc_ref[...].T
        if final:
            out_ref[...] = _rms(x, g_ref[...])
        else:
            xo_ref[...] = x
            _write_norms(x, g_ref[...], perm_ref, hn_ref, hd_refs)


def peer_dense(h_t, u, v, e2, r2, w, n, x, g_next, final, batch, seq, tm=512, te=1024):
    d, t = h_t.shape
    tiles = seq // tm
    first_keys = te // PEER_KEYS
    n_tiles = PEER_N_EXPERTS // te
    tab = pl.BlockSpec((PEER_HEADS, PEER_KEYS, tm), lambda i, e: (0, 0, i), pipeline_mode=pl.Buffered(1))
    tok = pl.BlockSpec((tm, d), lambda i, e: (i, 0))
    v_tiles = transposed_tiles_bf16(v, te)
    operands = [h_t, u, v_tiles, e2, r2, w, n, x, g_next.reshape(1, d)]
    in_specs = [pl.BlockSpec((d, tm), lambda i, e: (0, i)),
                pl.BlockSpec((te, d), lambda i, e: (e, 0)),
                pl.BlockSpec((None, d, te), lambda i, e: (e, 0, 0)),
                tab, tab, tab, tab,
                pl.BlockSpec((tm, d), lambda i, e: (i, 0), pipeline_mode=pl.Buffered(1)),
                _resident((1, d))]
    once = pl.Buffered(1)
    tok_out = pl.BlockSpec((tm, d), lambda i, e: (i, 0), pipeline_mode=once)
    if final:
        out_specs = tok_out
        out_shape = jax.ShapeDtypeStruct((t, d), F32)
    else:
        specs, shapes = _norm_specs(batch, seq, d, tm, lambda i, e: (i // tiles, i % tiles), once)
        out_specs = [tok_out] + specs
        out_shape = [jax.ShapeDtypeStruct((t, d), F32)] + shapes
        perm = jnp.asarray(_group_by_residue_matrices(tm), BF16)
        operands.append(perm)
        in_specs.append(_resident(perm.shape))
    return pl.pallas_call(
        functools.partial(_peer_dense_kernel, first_keys=first_keys, final=final),
        grid=(t // tm, n_tiles),
        in_specs=in_specs,
        out_specs=out_specs,
        out_shape=out_shape,
        scratch_shapes=[pltpu.VMEM((d, tm), F32)],
        compiler_params=_params(2),
        name="peer_dense_final" if final else "peer_dense",
    )(*operands)


W_IN_GATE_COL = IN_WIDTH - GATE_WIDTH
W_IN_Q_COL = 2 * A_WIDTH + B_WIDTH


def _main_weight_block(j):
    gate_blocks = GATE_WIDTH // QKV_GROUP_WIDTH
    return jnp.where(j < gate_blocks, j + W_IN_GATE_COL // QKV_GROUP_WIDTH, j - gate_blocks)


def _qkv_weights(w, g):
    cols = [W_IN_Q_COL + part * C_QKV_WIDTH + g * C_OUT_WIDTH for part in range(3)]
    return jnp.concatenate([w[:, c:c + C_OUT_WIDTH] for c in cols], axis=1).astype(BF16)


def kernel(x, rel_bias_table, norm_mix_g, w_in, a_norm_g, a_w_s, a_b_s, b_w_pool, b_scale, w_br_a,
           w_br_b, w_br_c, w_out, norm_ffn_g, peer_w_q, peer_sub_keys_1, peer_sub_keys_2, peer_u,
           peer_v, final_norm_g):
    batch, seq, d = x.shape
    t = batch * seq
    xf = x.reshape(t, d)
    hs = rmsnorm_rows(xf, norm_mix_g[0], batch, seq)
    out = None
    for l in range(DEPTH):
        p = in_proj(hs[0], to_bf16(w_in[l], 128), MAIN_WIDTH, _main_weight_block)
        ya = mixer_a(p, a_norm_g[l], a_w_s[l], a_b_s[l])
        yb = mixer_b(p, b_w_pool[l].astype(BF16), b_scale[l], seq)
        outs, lses = [], []
        for g in range(C_GROUPS):
            qkv = in_proj(hs[g].reshape(t, d), _qkv_weights(w_in[l], g))
            o, lse = dilated_attention_group(qkv, 0, rel_bias_table, g, batch, seq)
            outs.append(o)
            lses.append(lse)
        x_mid, h_ffn = merge_and_project(
            xf, p, ya, yb, outs, lses, w_br_a[l].astype(BF16), w_br_b[l].astype(BF16),
            w_br_c[l].astype(BF16), w_out[l].astype(BF16), norm_ffn_g[l], batch, seq)
        w_scores = fold_sub_keys(peer_w_q[l].T, peer_sub_keys_1[l], peer_sub_keys_2[l])
        e2, r2, w, n = peer_select(h_ffn, w_scores)
        final = l == DEPTH - 1
        g_next = final_norm_g if final else norm_mix_g[l + 1]
        res = peer_dense(h_ffn, to_bf16(peer_u[l], 1024), peer_v[l], e2, r2, w, n,
                         x_mid, g_next, final, batch, seq)
        if final:
            out = res
        else:
            xf, hs = res[0], res[1:]
    return out.reshape(batch, seq, d)
```

```python
import functools
import math

import numpy as np
import jax
import jax.numpy as jnp
from jax import lax
from jax.experimental import pallas as pl
from jax.experimental.pallas import tpu as pltpu

D_MODEL = 2048
DEPTH = 2
BLOCK = 128
EPS = 1e-6
A_HEADS = 8
A_HEAD_DIM = 128
A_WIDTH = A_HEADS * A_HEAD_DIM
POOL_WINDOWS = (2, 4, 8, 16)
B_GROUP_DIM = 256
B_WIDTH = len(POOL_WINDOWS) * B_GROUP_DIM
C_PATTERNS = ((128, 1), (512, 4), (2048, 16))
C_GROUPS = len(C_PATTERNS)
C_HEADS_PER_GROUP = 4
C_HEAD_DIM = 128
C_HEADS = C_GROUPS * C_HEADS_PER_GROUP
C_QKV_WIDTH = C_HEADS * C_HEAD_DIM
C_OUT_WIDTH = C_HEADS_PER_GROUP * C_HEAD_DIM
N_BRANCHES = 3
GATE_WIDTH = N_BRANCHES * D_MODEL
IN_WIDTH = 2 * A_WIDTH + B_WIDTH + 3 * C_QKV_WIDTH + GATE_WIDTH
REL_BUCKETS = 32
REL_MAX_DISTANCE = 2048
PEER_HEADS = 8
PEER_KEYS = 128
PEER_N_EXPERTS = PEER_KEYS * PEER_KEYS
PEER_KEY_DIM = 256
PEER_TOPK = 16
PEER_EXPERT_TILE = 1024

COL_GATE = 0
COL_AU = GATE_WIDTH
COL_AV = COL_AU + A_WIDTH
COL_B = COL_AV + A_WIDTH
MAIN_WIDTH = COL_B + B_WIDTH
QKV_GROUP_WIDTH = 3 * C_OUT_WIDTH

NEG_MASK = -1e30
BF16 = jnp.bfloat16
F32 = jnp.float32

VMEM_LIMIT_BYTES = 56 * 1024 * 1024
LANES = 128
PACKED_ROWS = 16
MXU_COLS = 256


def _params(n_axes):
    return pltpu.CompilerParams(
        dimension_semantics=("arbitrary",) * n_axes, vmem_limit_bytes=VMEM_LIMIT_BYTES)


def _rms(x, g):
    return x * lax.rsqrt(jnp.mean(x * x, axis=-1, keepdims=True) + EPS) * g


def _gelu(x):
    return 0.5 * x * (1.0 + lax.erf(x * (1.0 / math.sqrt(2.0))))


def _resident(shape):
    nd = len(shape)
    return pl.BlockSpec(shape, lambda *_: (0,) * nd, pipeline_mode=pl.Buffered(1))


DILATIONS = tuple(d for _, d in C_PATTERNS if d > 1)


def _group_by_residue_matrices(tm):
    mats = []
    for dil in DILATIONS:
        dst = np.arange(tm)
        src = (dst % (tm // dil)) * dil + dst // (tm // dil)
        mats.append((src[:, None] == np.arange(tm)[None, :]).astype(np.float32))
    return np.stack(mats)


def _write_norms(x, g, perm_ref, h_ref, hd_refs):
    h = _rms(x, g).astype(h_ref.dtype)
    h_ref[...] = h
    for k, ref in enumerate(hd_refs):
        grouped = jnp.dot(perm_ref[k], h, preferred_element_type=F32).astype(ref.dtype)
        ref[...] = grouped.reshape(ref.shape)


def _norm_specs(batch, seq, d, tm, index, buffering=None):
    specs = [pl.BlockSpec((tm, d), lambda *ids: (index(*ids)[0] * (seq // tm) + index(*ids)[1], 0),
                          pipeline_mode=buffering)]
    shapes = [jax.ShapeDtypeStruct((batch * seq, d), BF16)]
    for dil in DILATIONS:
        specs.append(pl.BlockSpec((None, dil, tm // dil, d),
                                  lambda *ids: (index(*ids)[0], 0, index(*ids)[1], 0),
                                  pipeline_mode=buffering))
        shapes.append(jax.ShapeDtypeStruct((batch, dil, seq // dil, d), BF16))
    return specs, shapes


def _norm_kernel(x_ref, g_ref, perm_ref, h_ref, *hd_refs):
    _write_norms(x_ref[...], g_ref[...], perm_ref, h_ref, hd_refs)


def rmsnorm_rows(x, g, batch, seq, tm=512):
    t, d = x.shape
    tiles = seq // tm
    specs, shapes = _norm_specs(batch, seq, d, tm, lambda i: (i // tiles, i % tiles))
    perm = jnp.asarray(_group_by_residue_matrices(tm), BF16)
    return pl.pallas_call(
        _norm_kernel,
        grid=(t // tm,),
        in_specs=[pl.BlockSpec((tm, d), lambda i: (i, 0)), _resident((1, d)), _resident(perm.shape)],
        out_specs=specs,
        out_shape=shapes,
        compiler_params=_params(1),
        name="rmsnorm_rows",
    )(x, g.reshape(1, d), perm)


def _cast_kernel(x_ref, o_ref):
    o_ref[...] = x_ref[...].astype(o_ref.dtype)


def to_bf16(stack, layer, rows):
    _, n, d = stack.shape
    return pl.pallas_call(
        _cast_kernel,
        grid=(n // rows,),
        in_specs=[pl.BlockSpec((None, rows, d), lambda i: (layer, i, 0))],
        out_specs=pl.BlockSpec((rows, d), lambda i: (i, 0)),
        out_shape=jax.ShapeDtypeStruct((n, d), BF16),
        compiler_params=_params(1),
        name="to_bf16",
    )(stack)


def _transpose_cast_kernel(x_ref, o_ref):
    o_ref[...] = x_ref[...].T.astype(o_ref.dtype)


def transposed_tiles_bf16(stack, layer, rows):
    _, n, d = stack.shape
    return pl.pallas_call(
        _transpose_cast_kernel,
        grid=(n // rows,),
        in_specs=[pl.BlockSpec((None, rows, d), lambda i: (layer, i, 0))],
        out_specs=pl.BlockSpec((None, d, rows), lambda i: (i, 0, 0)),
        out_shape=jax.ShapeDtypeStruct((n // rows, d, rows), BF16),
        compiler_params=_params(1),
        name="transposed_tiles_bf16",
    )(stack)


def _matmul_kernel(h_ref, w_ref, o_ref):
    o_ref[...] = jnp.dot(h_ref[...], w_ref[...], preferred_element_type=F32).astype(o_ref.dtype)


def in_proj(h, w, n_out=None, w_block=lambda j: j, tm=1024, tn=QKV_GROUP_WIDTH):
    t, d = h.shape
    n = w.shape[1] if n_out is None else n_out
    return pl.pallas_call(
        _matmul_kernel,
        grid=(n // tn, t // tm),
        in_specs=[pl.BlockSpec((tm, d), lambda j, i: (i, 0)),
                  pl.BlockSpec((d, tn), lambda j, i: (0, w_block(j)))],
        out_specs=pl.BlockSpec((tm, tn), lambda j, i: (i, j)),
        out_shape=jax.ShapeDtypeStruct((t, n), BF16),
        compiler_params=_params(2),
        name="in_proj",
    )(h, w)


def _mixer_a_kernel(au_ref, av_ref, g_ref, ws_ref, bs_ref, o_ref, *, chunks):
    u = _gelu(au_ref[...].astype(F32))
    v = _rms(_gelu(av_ref[...].astype(F32)), g_ref[...]).astype(BF16)
    row = lax.broadcasted_iota(jnp.int32, (BLOCK, BLOCK), 0)
    col = lax.broadcasted_iota(jnp.int32, (BLOCK, BLOCK), 1)
    causal = col <= row
    for hd in range(A_HEADS):
        w = jnp.where(causal, ws_ref[hd], 0.0).astype(BF16)
        bias = bs_ref[hd]
        cs = slice(hd * A_HEAD_DIM, (hd + 1) * A_HEAD_DIM)
        for c in range(chunks):
            rs = slice(c * BLOCK, (c + 1) * BLOCK)
            s = jnp.dot(w, v[rs, cs], preferred_element_type=F32) + bias
            o_ref[rs, cs] = (u[rs, cs] * s).astype(o_ref.dtype)


def mixer_a(p, a_norm_g, a_w_s, a_b_s, chunks=4):
    t = p.shape[0]
    tm = chunks * BLOCK
    bias = jnp.broadcast_to(a_b_s[:, :, None], (A_HEADS, BLOCK, A_HEAD_DIM))
    return pl.pallas_call(
        functools.partial(_mixer_a_kernel, chunks=chunks),
        grid=(t // tm,),
        in_specs=[pl.BlockSpec((tm, A_WIDTH), lambda i: (i, COL_AU // A_WIDTH)),
                  pl.BlockSpec((tm, A_WIDTH), lambda i: (i, COL_AV // A_WIDTH)),
                  _resident((1, A_WIDTH)),
                  _resident((A_HEADS, BLOCK, BLOCK)),
                  _resident((A_HEADS, BLOCK, A_HEAD_DIM))],
        out_specs=pl.BlockSpec((tm, A_WIDTH), lambda i: (i, 0)),
        out_shape=jax.ShapeDtypeStruct((t, A_WIDTH), BF16),
        compiler_params=_params(1),
        name="mixer_a",
    )(p, p, a_norm_g.reshape(1, A_WIDTH), a_w_s, bias)


def _pool_band_matrices():
    t = np.arange(BLOCK)[:, None]
    k = np.arange(2 * BLOCK)[None, :]
    back = BLOCK + t - k
    return np.stack([((back >= 0) & (back < w)) for w in POOL_WINDOWS]).astype(np.float32)


def _mixer_b_kernel(prev_ref, cur_ref, band_ref, wp_ref, sc_ref, o_ref, *, chunks, tiles_per_seq):
    first = (pl.program_id(0) % tiles_per_seq) == 0
    prev = jnp.where(first, jnp.zeros_like(prev_ref[...]), prev_ref[...])
    tile_pos = (pl.program_id(0) % tiles_per_seq) * (chunks * BLOCK)
    for c in range(chunks):
        rs = slice(c * BLOCK, (c + 1) * BLOCK)
        before = prev if c == 0 else cur_ref[(c - 1) * BLOCK:c * BLOCK, :]
        cur = cur_ref[rs, :]
        both = jnp.concatenate([before, cur], axis=0)
        pos = tile_pos + c * BLOCK + lax.broadcasted_iota(jnp.int32, (BLOCK, B_GROUP_DIM), 0)
        for g, w in enumerate(POOL_WINDOWS):
            cs = slice(g * B_GROUP_DIM, (g + 1) * B_GROUP_DIM)
            wsum = jnp.dot(band_ref[g], both[:, cs], preferred_element_type=F32)
            count = jnp.minimum(pos + 1, w).astype(F32)
            diff = (wsum / count - cur[:, cs].astype(F32)).astype(BF16)
            y = jnp.dot(diff, wp_ref[g], preferred_element_type=F32) * sc_ref[:, cs]
            o_ref[rs, cs] = y.astype(o_ref.dtype)


def mixer_b(p, w_pool, scale, seq, chunks=4):
    t = p.shape[0]
    tm = chunks * BLOCK
    band = jnp.asarray(_pool_band_matrices(), BF16)
    col = COL_B // B_WIDTH
    return pl.pallas_call(
        functools.partial(_mixer_b_kernel, chunks=chunks, tiles_per_seq=seq // tm),
        grid=(t // tm,),
        in_specs=[pl.BlockSpec((BLOCK, B_WIDTH), lambda i: (jnp.maximum(i * chunks - 1, 0), col)),
                  pl.BlockSpec((tm, B_WIDTH), lambda i: (i, col)),
                  _resident((len(POOL_WINDOWS), BLOCK, 2 * BLOCK)),
                  _resident((len(POOL_WINDOWS), B_GROUP_DIM, B_GROUP_DIM)),
                  _resident((1, B_WIDTH))],
        out_specs=pl.BlockSpec((tm, B_WIDTH), lambda i: (i, 0)),
        out_shape=jax.ShapeDtypeStruct((t, B_WIDTH), BF16),
        compiler_params=_params(1),
        name="mixer_b",
    )(p, p, band, w_pool, scale.reshape(1, B_WIDTH))


def _rel_bucket(n):
    max_exact = REL_BUCKETS // 2
    n = np.asarray(n)
    nl = np.maximum(n, max_exact).astype(np.float64)
    large = max_exact + (np.log(nl / max_exact) / math.log(REL_MAX_DISTANCE / max_exact)
                         * (REL_BUCKETS - max_exact)).astype(np.int32)
    large = np.minimum(large, REL_BUCKETS - 1)
    return np.where(n < max_exact, n, large).astype(np.int32)


def _attn_bias(rel_table, g, dilation, n_off):
    qi = np.arange(BLOCK)[:, None]
    ki = np.arange(2 * BLOCK)[None, :]
    off = BLOCK + qi - ki
    ok = (off >= 0) & (off <= n_off)
    buckets = _rel_bucket(np.clip(off, 0, n_off) * dilation)
    onehot = (buckets.reshape(1, -1) == np.arange(REL_BUCKETS)[:, None]).astype(np.float32)
    heads = rel_table[:, g * C_HEADS_PER_GROUP:(g + 1) * C_HEADS_PER_GROUP]
    bias = jnp.dot(heads.T.astype(F32), jnp.asarray(onehot), precision=lax.Precision.HIGHEST)
    bias = bias.reshape(C_HEADS_PER_GROUP, BLOCK, 2 * BLOCK)
    return jnp.where(ok[None], bias, NEG_MASK)


ATTN_SUBS_PER_STEP = 4


def _attn_kernel(q_ref, kp_ref, kc_ref, vp_ref, vc_ref, bias_ref, o_ref, lse_ref):
    first = pl.program_id(1) == 0
    key_is_prev = lax.broadcasted_iota(jnp.int32, (BLOCK, 2 * BLOCK), 1) < BLOCK
    hide = jnp.logical_and(first, key_is_prev)
    scale = C_HEAD_DIM ** -0.5
    for r in range(q_ref.shape[0]):
        for hd in range(C_HEADS_PER_GROUP):
            cs = slice(hd * C_HEAD_DIM, (hd + 1) * C_HEAD_DIM)
            keys = jnp.concatenate([kp_ref[r, :, cs], kc_ref[r, :, cs]], axis=0)
            vals = jnp.concatenate([vp_ref[r, :, cs], vc_ref[r, :, cs]], axis=0)
            logits = lax.dot_general(q_ref[r, :, cs], keys, (((1,), (1,)), ((), ())),
                                     preferred_element_type=F32) * scale
            logits = jnp.where(hide, NEG_MASK, logits + bias_ref[hd])
            m = jnp.max(logits, axis=-1, keepdims=True)
            e = jnp.exp(logits - m)
            s = jnp.sum(e, axis=-1, keepdims=True)
            o = jnp.dot(e.astype(BF16), vals, preferred_element_type=F32) / s
            o_ref[r, :, cs] = o.astype(o_ref.dtype)
            lse_ref[r, :, cs] = jnp.broadcast_to(m + jnp.log(s), (BLOCK, C_HEAD_DIM))


def dilated_attention_group(qkv, first_col, rel_table, g, batch, seq):
    window, dilation = C_PATTERNS[g]
    n_off = window // dilation
    sub_len = seq // dilation
    nb = sub_len // BLOCK
    subs = batch * dilation
    pv = qkv.reshape(subs, sub_len, qkv.shape[1])
    bias = _attn_bias(rel_table, g, dilation, n_off)
    c0 = first_col // C_OUT_WIDTH

    def col(k):
        return lambda s, n: (s, n, c0 + k)

    def col_prev(k):
        return lambda s, n: (s, jnp.maximum(n - 1, 0), c0 + k)

    per_step = math.gcd(subs, ATTN_SUBS_PER_STEP)
    blk = (per_step, BLOCK, C_OUT_WIDTH)
    out_blk = pl.BlockSpec(blk, lambda s, n: (s, n, 0))
    o, lse = pl.pallas_call(
        _attn_kernel,
        grid=(subs // per_step, nb),
        in_specs=[pl.BlockSpec(blk, col(0)),
                  pl.BlockSpec(blk, col_prev(1)), pl.BlockSpec(blk, col(1)),
                  pl.BlockSpec(blk, col_prev(2)), pl.BlockSpec(blk, col(2)),
                  _resident((C_HEADS_PER_GROUP, BLOCK, 2 * BLOCK))],
        out_specs=[out_blk, out_blk],
        out_shape=[jax.ShapeDtypeStruct((subs, sub_len, C_OUT_WIDTH), BF16),
                   jax.ShapeDtypeStruct((subs, sub_len, C_OUT_WIDTH), F32)],
        compiler_params=_params(2),
        name=f"dilated_attention_{g}",
    )(pv, pv, pv, pv, pv, bias)
    return o, lse


def _merge_kernel(x_ref, gate_ref, ya_ref, yb_ref, o0_ref, o1_ref, o2_ref, l0_ref, l1_ref, l2_ref,
                  wa_ref, wb_ref, wc_ref, wo_ref, g_ref, perm_ref, xo_ref, h_ref):
    tm = x_ref.shape[0]

    def token_order(ref, k, precision=None):
        return jnp.dot(perm_ref[k].astype(ref.dtype), ref[...].reshape(tm, C_OUT_WIDTH),
                       preferred_element_type=F32, precision=precision)

    exact_f32 = lax.Precision.HIGHEST
    l0, l1, l2 = l0_ref[...], token_order(l1_ref, 0, exact_f32), token_order(l2_ref, 1, exact_f32)
    m = jnp.maximum(jnp.maximum(l0, l1), l2)
    e0, e1, e2 = jnp.exp(l0 - m), jnp.exp(l1 - m), jnp.exp(l2 - m)
    yc = (e0 * o0_ref[...].astype(F32) + e1 * token_order(o1_ref, 0)
          + e2 * token_order(o2_ref, 1)) / (e0 + e1 + e2)

    def branch(k, y, w_ref):
        gate = jax.nn.sigmoid(gate_ref[:, k * D_MODEL:(k + 1) * D_MODEL].astype(F32))
        return gate * jnp.dot(y, w_ref[...], preferred_element_type=F32)

    merged = (branch(0, ya_ref[...], wa_ref) + branch(1, yb_ref[...], wb_ref)
              + branch(2, yc.astype(BF16), wc_ref))
    x = x_ref[...] + jnp.dot(merged.astype(BF16), wo_ref[...], preferred_element_type=F32)
    xo_ref[...] = x
    h_ref[...] = _rms(x, g_ref[...]).T.astype(h_ref.dtype)


def merge_and_project(x, p, ya, yb, outs, lses, wa, wb, wc, wo, g_ffn, batch, seq, tm=256):
    t, d = x.shape
    tiles = seq // tm
    row = lambda w: pl.BlockSpec((tm, w), lambda i: (i, 0))

    def grouped(dil):
        return pl.BlockSpec((None, dil, tm // dil, C_OUT_WIDTH), lambda i: (i // tiles, 0, i % tiles, 0))

    def by_residue(a, dil):
        return a.reshape(batch, dil, seq // dil, C_OUT_WIDTH)

    d1, d2 = DILATIONS
    perm = jnp.asarray(_group_by_residue_matrices(tm).transpose(0, 2, 1), BF16)
    return pl.pallas_call(
        _merge_kernel,
        grid=(t // tm,),
        in_specs=[row(d), row(GATE_WIDTH), row(A_WIDTH), row(B_WIDTH),
                  row(C_OUT_WIDTH), grouped(d1), grouped(d2),
                  row(C_OUT_WIDTH), grouped(d1), grouped(d2),
                  _resident((A_WIDTH, d)), _resident((B_WIDTH, d)), _resident((C_OUT_WIDTH, d)),
                  _resident((d, d)), _resident((1, d)), _resident(perm.shape)],
        out_specs=[row(d), pl.BlockSpec((d, tm), lambda i: (0, i))],
        out_shape=[jax.ShapeDtypeStruct((t, d), F32), jax.ShapeDtypeStruct((d, t), BF16)],
        compiler_params=_params(1),
        name="merge_and_project",
    )(x, p, ya, yb,
      outs[0].reshape(t, C_OUT_WIDTH), by_residue(outs[1], d1), by_residue(outs[2], d2),
      lses[0].reshape(t, C_OUT_WIDTH), by_residue(lses[1], d1), by_residue(lses[2], d2),
      wa, wb, wc, wo, g_ffn.reshape(1, d), perm)


def _candidate_pairs():
    return [(a, b) for a in range(PEER_TOPK) for b in range(PEER_TOPK // (a + 1))]


CAND_ROWS = 64


SUBLANES = 8


def _sorting_network(n):
    pairs, p = [], 1
    while p < n:
        k = p
        while k >= 1:
            for j in range(k % p, n - k, 2 * k):
                for i in range(min(k, n - j - k)):
                    if (i + j) // (2 * p) == (i + j + k) // (2 * p):
                        pairs.append((i + j, i + j + k))
            k //= 2
        p *= 2
    return pairs


def _top16_desc(slabs):
    lst = list(slabs)
    for a, b in _sorting_network(len(lst)):
        lst[a], lst[b] = jnp.maximum(lst[a], lst[b]), jnp.minimum(lst[a], lst[b])
    out = []
    for k in range(PEER_TOPK):
        head = jnp.max(lst[0], axis=0, keepdims=True)
        out.append(head)
        if k == PEER_TOPK - 1:
            break
        taken = lst[0] == head
        keep = min(len(lst), PEER_TOPK - k - 1)
        lst = [jnp.where(taken, lst[a + 1] if a + 1 < len(lst) else -jnp.inf, lst[a])
               for a in range(keep)]
    return out


def _row_slabs(x):
    return [x[r:r + SUBLANES, :] for r in range(0, x.shape[0], SUBLANES)]


COUNT_SPLIT_B = 4
COUNT_SPLIT_A = 3
assert (COUNT_SPLIT_A + 1) * (COUNT_SPLIT_B + 1) > PEER_TOPK


def _fold_keys_kernel(k_ref, wq_ref, o_ref):
    o_ref[...] = jnp.dot(k_ref[...], wq_ref[...], preferred_element_type=F32,
                         precision=lax.Precision.HIGHEST).astype(o_ref.dtype)


def fold_sub_keys(wq_t, k1, k2):
    n, d = wq_t.shape
    half = PEER_KEY_DIM // 2
    keys = jnp.stack([k1, k2])
    return pl.pallas_call(
        _fold_keys_kernel,
        grid=(n // half,),
        in_specs=[pl.BlockSpec((None, PEER_KEYS, half), lambda i: (i % 2, 0, 0)),
                  pl.BlockSpec((half, d), lambda i: (i, 0))],
        out_specs=pl.BlockSpec((PEER_KEYS, d), lambda i: (i, 0)),
        out_shape=jax.ShapeDtypeStruct((n // half * PEER_KEYS, d), BF16),
        compiler_params=_params(1),
        name="fold_sub_keys",
    )(keys, wq_t)


def _peer_select_kernel(h_ref, ws_ref, e2_ref, r2_ref, w_ref, n_ref, cand_ref, v2_ref):
    tm = h_ref.shape[1]
    scores = jnp.dot(ws_ref[...], h_ref[...], preferred_element_type=F32)
    cand_ref[...] = jnp.full(cand_ref.shape, -jnp.inf, F32)
    for hd in range(PEER_HEADS):
        s1 = scores[(2 * hd) * PEER_KEYS:(2 * hd + 1) * PEER_KEYS, :]
        s2 = scores[(2 * hd + 1) * PEER_KEYS:(2 * hd + 2) * PEER_KEYS, :]
        v1 = _top16_desc(_row_slabs(s1))
        v2 = _top16_desc(_row_slabs(s2))
        for b in range(PEER_TOPK):
            v2_ref[b:b + 1, :] = v2[b]
        for idx, (a, b) in enumerate(_candidate_pairs()):
            cand_ref[idx:idx + 1, :] = v1[a] + v2[b]
        best = _top16_desc(_row_slabs(cand_ref[...]))
        tau = best[PEER_TOPK - 1]
        z = functools.reduce(lambda acc, c: acc + jnp.exp(c - best[0]), best[1:],
                             jnp.ones_like(best[0]))
        rank2 = jnp.zeros(s2.shape, F32)
        for b in range(PEER_TOPK):
            rank2 = jnp.where(v2[b] > s2, float(b + 1), rank2)
        count = jnp.zeros(s1.shape, F32)
        for b in range(COUNT_SPLIT_B):
            count = jnp.where(s1 + v2[b] >= tau, float(b + 1), count)
        v2_all = v2_ref[...]
        low_rank = lax.broadcasted_iota(jnp.int32, v2_all.shape, 0) >= COUNT_SPLIT_B
        for a in range(COUNT_SPLIT_A):
            reached = jnp.logical_and(v1[a] + v2_all >= tau, low_rank)
            extra = jnp.sum(jnp.where(reached, 1.0, 0.0), axis=0, keepdims=True)
            count = count + jnp.where(s1 == v1[a], extra, 0.0)
        e2_ref[hd] = jnp.exp(s2 - v2[0]).astype(e2_ref.dtype)
        r2_ref[hd] = rank2.astype(r2_ref.dtype)
        w_ref[hd] = jnp.exp(s1 - v1[0]) / z
        n_ref[hd] = count


def peer_select(h_t, w_scores, tm=512):
    d, t = h_t.shape
    tab = lambda dt: jax.ShapeDtypeStruct((PEER_HEADS, PEER_KEYS, t), dt)
    blk = pl.BlockSpec((PEER_HEADS, PEER_KEYS, tm), lambda i: (0, 0, i))
    return pl.pallas_call(
        _peer_select_kernel,
        grid=(t // tm,),
        in_specs=[pl.BlockSpec((d, tm), lambda i: (0, i)),
                  _resident(w_scores.shape)],
        out_specs=[blk, blk, blk, blk],
        out_shape=[tab(BF16), tab(BF16), tab(F32), tab(F32)],
        scratch_shapes=[pltpu.VMEM((CAND_ROWS, tm), F32), pltpu.VMEM((PEER_TOPK, tm), F32)],
        compiler_params=_params(1),
        name="peer_select",
    )(h_t, w_scores)


def _peer_dense_kernel(h_ref, u_ref, vt_ref, e2_ref, r2_ref, w_ref, n_ref, x_ref, g_ref,
                       *rest, first_keys, final):
    if final:
        out_ref, acc_ref = rest
    else:
        perm_ref, xo_ref, hn_ref, *hd_refs, acc_ref = rest
    e = pl.program_id(1)
    tm = h_ref.shape[1]

    @pl.when(e == 0)
    def _():
        acc_ref[...] = jnp.zeros_like(acc_ref)

    first_key = e * first_keys
    gates = []
    for ii in range(first_keys):
        g = None
        for hd in range(PEER_HEADS):
            n_row = jnp.broadcast_to(n_ref[hd, pl.ds(first_key + ii, 1), :].astype(BF16), (PEER_KEYS, tm))
            w_row = jnp.broadcast_to(w_ref[hd, pl.ds(first_key + ii, 1), :].astype(BF16), (PEER_KEYS, tm))
            picked = jnp.where(r2_ref[hd] < n_row, e2_ref[hd], jnp.zeros((), BF16)) * w_row
            g = picked if g is None else g + picked
        gates.append(g)
    gate_t = jnp.concatenate(gates, axis=0)

    a_t = jnp.dot(u_ref[...], h_ref[...], preferred_element_type=F32)
    act_t = _gelu(a_t).astype(BF16) * gate_t
    acc_ref[...] += jnp.dot(vt_ref[...], act_t, preferred_element_type=F32)

    @pl.when(e == pl.num_programs(1) - 1)
    def _():
        x = x_ref[...] + acc_ref[...].T
        if final:
            out_ref[...] = _rms(x, g_ref[...])
        else:
            xo_ref[...] = x
            _write_norms(x, g_ref[...], perm_ref, hn_ref, hd_refs)


def peer_dense(h_t, u, v_tiles, e2, r2, w, n, x, g_next, final, batch, seq, tm=512):
    d, t = h_t.shape
    tiles = seq // tm
    n_tiles, _, te = v_tiles.shape
    first_keys = te // PEER_KEYS
    tab = pl.BlockSpec((PEER_HEADS, PEER_KEYS, tm), lambda i, e: (0, 0, i), pipeline_mode=pl.Buffered(1))
    tok = pl.BlockSpec((tm, d), lambda i, e: (i, 0))
    operands = [h_t, u, v_tiles, e2, r2, w, n, x, g_next.reshape(1, d)]
    in_specs = [pl.BlockSpec((d, tm), lambda i, e: (0, i)),
                pl.BlockSpec((te, d), lambda i, e: (e, 0)),
                pl.BlockSpec((None, d, te), lambda i, e: (e, 0, 0)),
                tab, tab, tab, tab,
                pl.BlockSpec((tm, d), lambda i, e: (i, 0), pipeline_mode=pl.Buffered(1)),
                _resident((1, d))]
    once = pl.Buffered(1)
    tok_out = pl.BlockSpec((tm, d), lambda i, e: (i, 0), pipeline_mode=once)
    if final:
        out_specs = tok_out
        out_shape = jax.ShapeDtypeStruct((t, d), F32)
    else:
        specs, shapes = _norm_specs(batch, seq, d, tm, lambda i, e: (i // tiles, i % tiles), once)
        out_specs = [tok_out] + specs
        out_shape = [jax.ShapeDtypeStruct((t, d), F32)] + shapes
        perm = jnp.asarray(_group_by_residue_matrices(tm), BF16)
        operands.append(perm)
        in_specs.append(_resident(perm.shape))
    return pl.pallas_call(
        functools.partial(_peer_dense_kernel, first_keys=first_keys, final=final),
        grid=(t // tm, n_tiles),
        in_specs=in_specs,
        out_specs=out_specs,
        out_shape=out_shape,
        scratch_shapes=[pltpu.VMEM((d, tm), F32)],
        compiler_params=_params(2),
        name="peer_dense_final" if final else "peer_dense",
    )(*operands)


W_IN_GATE_COL = IN_WIDTH - GATE_WIDTH
W_IN_Q_COL = 2 * A_WIDTH + B_WIDTH


def _main_weight_block(j):
    gate_blocks = GATE_WIDTH // QKV_GROUP_WIDTH
    return jnp.where(j < gate_blocks, j + W_IN_GATE_COL // QKV_GROUP_WIDTH, j - gate_blocks)


def _qkv_weights(w, g):
    cols = [W_IN_Q_COL + part * C_QKV_WIDTH + g * C_OUT_WIDTH for part in range(3)]
    return jnp.concatenate([w[:, c:c + C_OUT_WIDTH] for c in cols], axis=1).astype(BF16)


def kernel(x, rel_bias_table, norm_mix_g, w_in, a_norm_g, a_w_s, a_b_s, b_w_pool, b_scale, w_br_a,
           w_br_b, w_br_c, w_out, norm_ffn_g, peer_w_q, peer_sub_keys_1, peer_sub_keys_2, peer_u,
           peer_v, final_norm_g):
    batch, seq, d = x.shape
    t = batch * seq
    xf = x.reshape(t, d)
    hs = rmsnorm_rows(xf, norm_mix_g[0], batch, seq)
    out = None
    for l in range(DEPTH):
        p = in_proj(hs[0], to_bf16(w_in, l, 128), MAIN_WIDTH, _main_weight_block)
        ya = mixer_a(p, a_norm_g[l], a_w_s[l], a_b_s[l])
        yb = mixer_b(p, b_w_pool[l].astype(BF16), b_scale[l], seq)
        outs, lses = [], []
        for g in range(C_GROUPS):
            qkv = in_proj(hs[g].reshape(t, d), _qkv_weights(w_in[l], g))
            o, lse = dilated_attention_group(qkv, 0, rel_bias_table, g, batch, seq)
            outs.append(o)
            lses.append(lse)
        x_mid, h_ffn = merge_and_project(
            xf, p, ya, yb, outs, lses, w_br_a[l].astype(BF16), w_br_b[l].astype(BF16),
            w_br_c[l].astype(BF16), w_out[l].astype(BF16), norm_ffn_g[l], batch, seq)
        w_scores = fold_sub_keys(peer_w_q[l].T, peer_sub_keys_1[l], peer_sub_keys_2[l])
        e2, r2, w, n = peer_select(h_ffn, w_scores)
        final = l == DEPTH - 1
        g_next = final_norm_g if final else norm_mix_g[l + 1]
        res = peer_dense(h_ffn, to_bf16(peer_u, l, PEER_EXPERT_TILE),
                         transposed_tiles_bf16(peer_v, l, PEER_EXPERT_TILE), e2, r2, w, n,
                         x_mid, g_next, final, batch, seq)
        if final:
            out = res
        else:
            xf, hs = res[0], res[1:]
    return out.reshape(batch, seq, d)
```

```python
import functools
import math

import numpy as np
import jax
import jax.numpy as jnp
from jax import lax
from jax.experimental import pallas as pl
from jax.experimental.pallas import tpu as pltpu

D_MODEL = 2048
DEPTH = 2
BLOCK = 128
EPS = 1e-6
A_HEADS = 8
A_HEAD_DIM = 128
A_WIDTH = A_HEADS * A_HEAD_DIM
POOL_WINDOWS = (2, 4, 8, 16)
B_GROUP_DIM = 256
B_WIDTH = len(POOL_WINDOWS) * B_GROUP_DIM
C_PATTERNS = ((128, 1), (512, 4), (2048, 16))
C_GROUPS = len(C_PATTERNS)
C_HEADS_PER_GROUP = 4
C_HEAD_DIM = 128
C_HEADS = C_GROUPS * C_HEADS_PER_GROUP
C_QKV_WIDTH = C_HEADS * C_HEAD_DIM
C_OUT_WIDTH = C_HEADS_PER_GROUP * C_HEAD_DIM
N_BRANCHES = 3
GATE_WIDTH = N_BRANCHES * D_MODEL
IN_WIDTH = 2 * A_WIDTH + B_WIDTH + 3 * C_QKV_WIDTH + GATE_WIDTH
REL_BUCKETS = 32
REL_MAX_DISTANCE = 2048
PEER_HEADS = 8
PEER_KEYS = 128
PEER_N_EXPERTS = PEER_KEYS * PEER_KEYS
PEER_KEY_DIM = 256
PEER_TOPK = 16
PEER_EXPERT_TILE = 1024

COL_GATE = 0
COL_AU = GATE_WIDTH
COL_AV = COL_AU + A_WIDTH
COL_B = COL_AV + A_WIDTH
MAIN_WIDTH = COL_B + B_WIDTH
QKV_GROUP_WIDTH = 3 * C_OUT_WIDTH

NEG_MASK = -1e30
BF16 = jnp.bfloat16
F32 = jnp.float32

VMEM_LIMIT_BYTES = 60 * 1024 * 1024
LANES = 128
PACKED_ROWS = 16
MXU_COLS = 256


def _params(n_axes):
    return pltpu.CompilerParams(
        dimension_semantics=("arbitrary",) * n_axes, vmem_limit_bytes=VMEM_LIMIT_BYTES)


def _rms(x, g):
    return x * lax.rsqrt(jnp.mean(x * x, axis=-1, keepdims=True) + EPS) * g


def _gelu(x):
    return 0.5 * x * (1.0 + lax.erf(x * (1.0 / math.sqrt(2.0))))


def _resident(shape):
    nd = len(shape)
    return pl.BlockSpec(shape, lambda *_: (0,) * nd, pipeline_mode=pl.Buffered(1))


DILATIONS = tuple(d for _, d in C_PATTERNS if d > 1)


def _group_by_residue_matrices(tm):
    mats = []
    for dil in DILATIONS:
        dst = np.arange(tm)
        src = (dst % (tm // dil)) * dil + dst // (tm // dil)
        mats.append((src[:, None] == np.arange(tm)[None, :]).astype(np.float32))
    return np.stack(mats)


def _write_norms(x, g, perm_ref, h_ref, hd_refs):
    h = _rms(x, g).astype(h_ref.dtype)
    h_ref[...] = h
    for k, ref in enumerate(hd_refs):
        grouped = jnp.dot(perm_ref[k], h, preferred_element_type=F32).astype(ref.dtype)
        ref[...] = grouped.reshape(ref.shape)


def _norm_specs(batch, seq, d, tm, index, buffering=None):
    specs = [pl.BlockSpec((tm, d), lambda *ids: (index(*ids)[0] * (seq // tm) + index(*ids)[1], 0),
                          pipeline_mode=buffering)]
    shapes = [jax.ShapeDtypeStruct((batch * seq, d), BF16)]
    for dil in DILATIONS:
        specs.append(pl.BlockSpec((None, dil, tm // dil, d),
                                  lambda *ids: (index(*ids)[0], 0, index(*ids)[1], 0),
                                  pipeline_mode=buffering))
        shapes.append(jax.ShapeDtypeStruct((batch, dil, seq // dil, d), BF16))
    return specs, shapes


def _norm_kernel(x_ref, g_ref, perm_ref, h_ref, *hd_refs):
    _write_norms(x_ref[...], g_ref[...], perm_ref, h_ref, hd_refs)


def rmsnorm_rows(x, g, batch, seq, tm=512):
    t, d = x.shape
    tiles = seq // tm
    specs, shapes = _norm_specs(batch, seq, d, tm, lambda i: (i // tiles, i % tiles))
    perm = jnp.asarray(_group_by_residue_matrices(tm), BF16)
    return pl.pallas_call(
        _norm_kernel,
        grid=(t // tm,),
        in_specs=[pl.BlockSpec((tm, d), lambda i: (i, 0)), _resident((1, d)), _resident(perm.shape)],
        out_specs=specs,
        out_shape=shapes,
        compiler_params=_params(1),
        name="rmsnorm_rows",
    )(x, g.reshape(1, d), perm)


def _cast_kernel(x_ref, o_ref):
    o_ref[...] = x_ref[...].astype(o_ref.dtype)


def to_bf16(stack, layer, rows):
    _, n, d = stack.shape
    return pl.pallas_call(
        _cast_kernel,
        grid=(n // rows,),
        in_specs=[pl.BlockSpec((None, rows, d), lambda i: (layer, i, 0))],
        out_specs=pl.BlockSpec((rows, d), lambda i: (i, 0)),
        out_shape=jax.ShapeDtypeStruct((n, d), BF16),
        compiler_params=_params(1),
        name="to_bf16",
    )(stack)


def _transpose_cast_kernel(x_ref, o_ref):
    o_ref[...] = x_ref[...].T.astype(o_ref.dtype)


def transposed_tiles_bf16(stack, layer, rows):
    _, n, d = stack.shape
    return pl.pallas_call(
        _transpose_cast_kernel,
        grid=(n // rows,),
        in_specs=[pl.BlockSpec((None, rows, d), lambda i: (layer, i, 0))],
        out_specs=pl.BlockSpec((None, d, rows), lambda i: (i, 0, 0)),
        out_shape=jax.ShapeDtypeStruct((n // rows, d, rows), BF16),
        compiler_params=_params(1),
        name="transposed_tiles_bf16",
    )(stack)


def _matmul_kernel(h_ref, w_ref, o_ref):
    o_ref[...] = jnp.dot(h_ref[...], w_ref[...], preferred_element_type=F32).astype(o_ref.dtype)


def in_proj(h, w, n_out=None, w_block=lambda j: j, tm=1024, tn=QKV_GROUP_WIDTH):
    t, d = h.shape
    n = w.shape[1] if n_out is None else n_out
    return pl.pallas_call(
        _matmul_kernel,
        grid=(n // tn, t // tm),
        in_specs=[pl.BlockSpec((tm, d), lambda j, i: (i, 0)),
                  pl.BlockSpec((d, tn), lambda j, i: (0, w_block(j)))],
        out_specs=pl.BlockSpec((tm, tn), lambda j, i: (i, j)),
        out_shape=jax.ShapeDtypeStruct((t, n), BF16),
        compiler_params=_params(2),
        name="in_proj",
    )(h, w)


def _mixer_a_kernel(au_ref, av_ref, g_ref, ws_ref, bs_ref, o_ref, *, chunks):
    u = _gelu(au_ref[...].astype(F32))
    v = _rms(_gelu(av_ref[...].astype(F32)), g_ref[...]).astype(BF16)
    row = lax.broadcasted_iota(jnp.int32, (BLOCK, BLOCK), 0)
    col = lax.broadcasted_iota(jnp.int32, (BLOCK, BLOCK), 1)
    causal = col <= row
    for hd in range(A_HEADS):
        w = jnp.where(causal, ws_ref[hd], 0.0).astype(BF16)
        bias = bs_ref[hd]
        cs = slice(hd * A_HEAD_DIM, (hd + 1) * A_HEAD_DIM)
        for c in range(chunks):
            rs = slice(c * BLOCK, (c + 1) * BLOCK)
            s = jnp.dot(w, v[rs, cs], preferred_element_type=F32) + bias
            o_ref[rs, cs] = (u[rs, cs] * s).astype(o_ref.dtype)


def mixer_a(p, a_norm_g, a_w_s, a_b_s, chunks=4):
    t = p.shape[0]
    tm = chunks * BLOCK
    bias = jnp.broadcast_to(a_b_s[:, :, None], (A_HEADS, BLOCK, A_HEAD_DIM))
    return pl.pallas_call(
        functools.partial(_mixer_a_kernel, chunks=chunks),
        grid=(t // tm,),
        in_specs=[pl.BlockSpec((tm, A_WIDTH), lambda i: (i, COL_AU // A_WIDTH)),
                  pl.BlockSpec((tm, A_WIDTH), lambda i: (i, COL_AV // A_WIDTH)),
                  _resident((1, A_WIDTH)),
                  _resident((A_HEADS, BLOCK, BLOCK)),
                  _resident((A_HEADS, BLOCK, A_HEAD_DIM))],
        out_specs=pl.BlockSpec((tm, A_WIDTH), lambda i: (i, 0)),
        out_shape=jax.ShapeDtypeStruct((t, A_WIDTH), BF16),
        compiler_params=_params(1),
        name="mixer_a",
    )(p, p, a_norm_g.reshape(1, A_WIDTH), a_w_s, bias)


def _pool_band_matrices():
    t = np.arange(BLOCK)[:, None]
    k = np.arange(2 * BLOCK)[None, :]
    back = BLOCK + t - k
    return np.stack([((back >= 0) & (back < w)) for w in POOL_WINDOWS]).astype(np.float32)


def _mixer_b_kernel(prev_ref, cur_ref, band_ref, wp_ref, sc_ref, o_ref, *, chunks, tiles_per_seq):
    first = (pl.program_id(0) % tiles_per_seq) == 0
    prev = jnp.where(first, jnp.zeros_like(prev_ref[...]), prev_ref[...])
    tile_pos = (pl.program_id(0) % tiles_per_seq) * (chunks * BLOCK)
    for c in range(chunks):
        rs = slice(c * BLOCK, (c + 1) * BLOCK)
        before = prev if c == 0 else cur_ref[(c - 1) * BLOCK:c * BLOCK, :]
        cur = cur_ref[rs, :]
        both = jnp.concatenate([before, cur], axis=0)
        pos = tile_pos + c * BLOCK + lax.broadcasted_iota(jnp.int32, (BLOCK, B_GROUP_DIM), 0)
        for g, w in enumerate(POOL_WINDOWS):
            cs = slice(g * B_GROUP_DIM, (g + 1) * B_GROUP_DIM)
            wsum = jnp.dot(band_ref[g], both[:, cs], preferred_element_type=F32)
            count = jnp.minimum(pos + 1, w).astype(F32)
            diff = (wsum / count - cur[:, cs].astype(F32)).astype(BF16)
            y = jnp.dot(diff, wp_ref[g], preferred_element_type=F32) * sc_ref[:, cs]
            o_ref[rs, cs] = y.astype(o_ref.dtype)


def mixer_b(p, w_pool, scale, seq, chunks=4):
    t = p.shape[0]
    tm = chunks * BLOCK
    band = jnp.asarray(_pool_band_matrices(), BF16)
    col = COL_B // B_WIDTH
    return pl.pallas_call(
        functools.partial(_mixer_b_kernel, chunks=chunks, tiles_per_seq=seq // tm),
        grid=(t // tm,),
        in_specs=[pl.BlockSpec((BLOCK, B_WIDTH), lambda i: (jnp.maximum(i * chunks - 1, 0), col)),
                  pl.BlockSpec((tm, B_WIDTH), lambda i: (i, col)),
                  _resident((len(POOL_WINDOWS), BLOCK, 2 * BLOCK)),
                  _resident((len(POOL_WINDOWS), B_GROUP_DIM, B_GROUP_DIM)),
                  _resident((1, B_WIDTH))],
        out_specs=pl.BlockSpec((tm, B_WIDTH), lambda i: (i, 0)),
        out_shape=jax.ShapeDtypeStruct((t, B_WIDTH), BF16),
        compiler_params=_params(1),
        name="mixer_b",
    )(p, p, band, w_pool, scale.reshape(1, B_WIDTH))


def _rel_bucket(n):
    max_exact = REL_BUCKETS // 2
    n = np.asarray(n)
    nl = np.maximum(n, max_exact).astype(np.float64)
    large = max_exact + (np.log(nl / max_exact) / math.log(REL_MAX_DISTANCE / max_exact)
                         * (REL_BUCKETS - max_exact)).astype(np.int32)
    large = np.minimum(large, REL_BUCKETS - 1)
    return np.where(n < max_exact, n, large).astype(np.int32)


def _attn_bias(rel_table, g, dilation, n_off):
    qi = np.arange(BLOCK)[:, None]
    ki = np.arange(2 * BLOCK)[None, :]
    off = BLOCK + qi - ki
    ok = (off >= 0) & (off <= n_off)
    buckets = _rel_bucket(np.clip(off, 0, n_off) * dilation)
    onehot = (buckets.reshape(1, -1) == np.arange(REL_BUCKETS)[:, None]).astype(np.float32)
    heads = rel_table[:, g * C_HEADS_PER_GROUP:(g + 1) * C_HEADS_PER_GROUP]
    bias = jnp.dot(heads.T.astype(F32), jnp.asarray(onehot), precision=lax.Precision.HIGHEST)
    bias = bias.reshape(C_HEADS_PER_GROUP, BLOCK, 2 * BLOCK)
    return jnp.where(ok[None], bias, NEG_MASK)


ATTN_SUBS_PER_STEP = 4


def _attn_kernel(q_ref, kp_ref, kc_ref, vp_ref, vc_ref, bias_ref, o_ref, lse_ref):
    first = pl.program_id(1) == 0
    key_is_prev = lax.broadcasted_iota(jnp.int32, (BLOCK, 2 * BLOCK), 1) < BLOCK
    hide = jnp.logical_and(first, key_is_prev)
    scale = C_HEAD_DIM ** -0.5
    for r in range(q_ref.shape[0]):
        for hd in range(C_HEADS_PER_GROUP):
            cs = slice(hd * C_HEAD_DIM, (hd + 1) * C_HEAD_DIM)
            keys = jnp.concatenate([kp_ref[r, :, cs], kc_ref[r, :, cs]], axis=0)
            vals = jnp.concatenate([vp_ref[r, :, cs], vc_ref[r, :, cs]], axis=0)
            logits = lax.dot_general(q_ref[r, :, cs], keys, (((1,), (1,)), ((), ())),
                                     preferred_element_type=F32) * scale
            logits = jnp.where(hide, NEG_MASK, logits + bias_ref[hd])
            m = jnp.max(logits, axis=-1, keepdims=True)
            e = jnp.exp(logits - m)
            s = jnp.sum(e, axis=-1, keepdims=True)
            o = jnp.dot(e.astype(BF16), vals, preferred_element_type=F32) / s
            o_ref[r, :, cs] = o.astype(o_ref.dtype)
            lse_ref[r, :, cs] = jnp.broadcast_to(m + jnp.log(s), (BLOCK, C_HEAD_DIM))


def dilated_attention_group(qkv, first_col, rel_table, g, batch, seq):
    window, dilation = C_PATTERNS[g]
    n_off = window // dilation
    sub_len = seq // dilation
    nb = sub_len // BLOCK
    subs = batch * dilation
    pv = qkv.reshape(subs, sub_len, qkv.shape[1])
    bias = _attn_bias(rel_table, g, dilation, n_off)
    c0 = first_col // C_OUT_WIDTH

    def col(k):
        return lambda s, n: (s, n, c0 + k)

    def col_prev(k):
        return lambda s, n: (s, jnp.maximum(n - 1, 0), c0 + k)

    per_step = math.gcd(subs, ATTN_SUBS_PER_STEP)
    blk = (per_step, BLOCK, C_OUT_WIDTH)
    out_blk = pl.BlockSpec(blk, lambda s, n: (s, n, 0))
    o, lse = pl.pallas_call(
        _attn_kernel,
        grid=(subs // per_step, nb),
        in_specs=[pl.BlockSpec(blk, col(0)),
                  pl.BlockSpec(blk, col_prev(1)), pl.BlockSpec(blk, col(1)),
                  pl.BlockSpec(blk, col_prev(2)), pl.BlockSpec(blk, col(2)),
                  _resident((C_HEADS_PER_GROUP, BLOCK, 2 * BLOCK))],
        out_specs=[out_blk, out_blk],
        out_shape=[jax.ShapeDtypeStruct((subs, sub_len, C_OUT_WIDTH), BF16),
                   jax.ShapeDtypeStruct((subs, sub_len, C_OUT_WIDTH), F32)],
        compiler_params=_params(2),
        name=f"dilated_attention_{g}",
    )(pv, pv, pv, pv, pv, bias)
    return o, lse


def _merge_kernel(x_ref, gate_ref, ya_ref, yb_ref, o0_ref, o1_ref, o2_ref, l0_ref, l1_ref, l2_ref,
                  wa_ref, wb_ref, wc_ref, wo_ref, g_ref, perm_ref, xo_ref, h_ref):
    tm = x_ref.shape[0]

    def token_order(ref, k, precision=None):
        return jnp.dot(perm_ref[k].astype(ref.dtype), ref[...].reshape(tm, C_OUT_WIDTH),
                       preferred_element_type=F32, precision=precision)

    exact_f32 = lax.Precision.HIGHEST
    l0, l1, l2 = l0_ref[...], token_order(l1_ref, 0, exact_f32), token_order(l2_ref, 1, exact_f32)
    m = jnp.maximum(jnp.maximum(l0, l1), l2)
    e0, e1, e2 = jnp.exp(l0 - m), jnp.exp(l1 - m), jnp.exp(l2 - m)
    yc = (e0 * o0_ref[...].astype(F32) + e1 * token_order(o1_ref, 0)
          + e2 * token_order(o2_ref, 1)) / (e0 + e1 + e2)

    def branch(k, y, w_ref):
        gate = jax.nn.sigmoid(gate_ref[:, k * D_MODEL:(k + 1) * D_MODEL].astype(F32))
        return gate * jnp.dot(y, w_ref[...], preferred_element_type=F32)

    merged = (branch(0, ya_ref[...], wa_ref) + branch(1, yb_ref[...], wb_ref)
              + branch(2, yc.astype(BF16), wc_ref))
    x = x_ref[...] + jnp.dot(merged.astype(BF16), wo_ref[...], preferred_element_type=F32)
    xo_ref[...] = x
    h_ref[...] = _rms(x, g_ref[...]).T.astype(h_ref.dtype)


def merge_and_project(x, p, ya, yb, outs, lses, wa, wb, wc, wo, g_ffn, batch, seq, tm=256):
    t, d = x.shape
    tiles = seq // tm
    row = lambda w: pl.BlockSpec((tm, w), lambda i: (i, 0))

    def grouped(dil):
        return pl.BlockSpec((None, dil, tm // dil, C_OUT_WIDTH), lambda i: (i // tiles, 0, i % tiles, 0))

    def by_residue(a, dil):
        return a.reshape(batch, dil, seq // dil, C_OUT_WIDTH)

    d1, d2 = DILATIONS
    perm = jnp.asarray(_group_by_residue_matrices(tm).transpose(0, 2, 1), BF16)
    return pl.pallas_call(
        _merge_kernel,
        grid=(t // tm,),
        in_specs=[row(d), row(GATE_WIDTH), row(A_WIDTH), row(B_WIDTH),
                  row(C_OUT_WIDTH), grouped(d1), grouped(d2),
                  row(C_OUT_WIDTH), grouped(d1), grouped(d2),
                  _resident((A_WIDTH, d)), _resident((B_WIDTH, d)), _resident((C_OUT_WIDTH, d)),
                  _resident((d, d)), _resident((1, d)), _resident(perm.shape)],
        out_specs=[row(d), pl.BlockSpec((d, tm), lambda i: (0, i))],
        out_shape=[jax.ShapeDtypeStruct((t, d), F32), jax.ShapeDtypeStruct((d, t), BF16)],
        compiler_params=_params(1),
        name="merge_and_project",
    )(x, p, ya, yb,
      outs[0].reshape(t, C_OUT_WIDTH), by_residue(outs[1], d1), by_residue(outs[2], d2),
      lses[0].reshape(t, C_OUT_WIDTH), by_residue(lses[1], d1), by_residue(lses[2], d2),
      wa, wb, wc, wo, g_ffn.reshape(1, d), perm)


def _candidate_pairs():
    return [(a, b) for a in range(PEER_TOPK) for b in range(PEER_TOPK // (a + 1))]


CAND_ROWS = 64


SUBLANES = 8


def _sorting_network(n):
    pairs, p = [], 1
    while p < n:
        k = p
        while k >= 1:
            for j in range(k % p, n - k, 2 * k):
                for i in range(min(k, n - j - k)):
                    if (i + j) // (2 * p) == (i + j + k) // (2 * p):
                        pairs.append((i + j, i + j + k))
            k //= 2
        p *= 2
    return pairs


def _top16_desc(slabs):
    lst = list(slabs)
    for a, b in _sorting_network(len(lst)):
        lst[a], lst[b] = jnp.maximum(lst[a], lst[b]), jnp.minimum(lst[a], lst[b])
    out = []
    for k in range(PEER_TOPK):
        head = jnp.max(lst[0], axis=0, keepdims=True)
        out.append(head)
        if k == PEER_TOPK - 1:
            break
        taken = lst[0] == head
        keep = min(len(lst), PEER_TOPK - k - 1)
        lst = [jnp.where(taken, lst[a + 1] if a + 1 < len(lst) else -jnp.inf, lst[a])
               for a in range(keep)]
    return out


def _row_slabs(x):
    return [x[r:r + SUBLANES, :] for r in range(0, x.shape[0], SUBLANES)]


COUNT_SPLIT_B = 4
COUNT_SPLIT_A = 3
assert (COUNT_SPLIT_A + 1) * (COUNT_SPLIT_B + 1) > PEER_TOPK


def _fold_keys_kernel(k_ref, wq_ref, o_ref):
    o_ref[...] = lax.dot_general(k_ref[...], wq_ref[...], (((1,), (1,)), ((), ())),
                                 preferred_element_type=F32,
                                 precision=lax.Precision.HIGHEST).astype(o_ref.dtype)


def fold_sub_keys(w_q, layer, k1, k2):
    _, d, n = w_q.shape
    half = PEER_KEY_DIM // 2
    keys = jnp.stack([k1, k2])
    return pl.pallas_call(
        _fold_keys_kernel,
        grid=(n // half,),
        in_specs=[pl.BlockSpec((None, PEER_KEYS, half), lambda i: (i % 2, 0, 0)),
                  pl.BlockSpec((None, d, half), lambda i: (layer, 0, i))],
        out_specs=pl.BlockSpec((PEER_KEYS, d), lambda i: (i, 0)),
        out_shape=jax.ShapeDtypeStruct((n // half * PEER_KEYS, d), BF16),
        compiler_params=_params(1),
        name="fold_sub_keys",
    )(keys, w_q)


def _peer_select_kernel(h_ref, ws_ref, e2_ref, r2_ref, w_ref, n_ref, cand_ref, v2_ref):
    tm = h_ref.shape[1]
    scores = jnp.dot(ws_ref[...], h_ref[...], preferred_element_type=F32)
    cand_ref[...] = jnp.full(cand_ref.shape, -jnp.inf, F32)
    for hd in range(PEER_HEADS):
        s1 = scores[(2 * hd) * PEER_KEYS:(2 * hd + 1) * PEER_KEYS, :]
        s2 = scores[(2 * hd + 1) * PEER_KEYS:(2 * hd + 2) * PEER_KEYS, :]
        v1 = _top16_desc(_row_slabs(s1))
        v2 = _top16_desc(_row_slabs(s2))
        for b in range(PEER_TOPK):
            v2_ref[b:b + 1, :] = v2[b]
        for idx, (a, b) in enumerate(_candidate_pairs()):
            cand_ref[idx:idx + 1, :] = v1[a] + v2[b]
        best = _top16_desc(_row_slabs(cand_ref[...]))
        tau = best[PEER_TOPK - 1]
        z = functools.reduce(lambda acc, c: acc + jnp.exp(c - best[0]), best[1:],
                             jnp.ones_like(best[0]))
        rank2 = jnp.zeros(s2.shape, F32)
        for b in range(PEER_TOPK):
            rank2 = jnp.where(v2[b] > s2, float(b + 1), rank2)
        count = jnp.zeros(s1.shape, F32)
        for b in range(COUNT_SPLIT_B):
            count = jnp.where(s1 + v2[b] >= tau, float(b + 1), count)
        v2_all = v2_ref[...]
        low_rank = lax.broadcasted_iota(jnp.int32, v2_all.shape, 0) >= COUNT_SPLIT_B
        for a in range(COUNT_SPLIT_A):
            reached = jnp.logical_and(v1[a] + v2_all >= tau, low_rank)
            extra = jnp.sum(jnp.where(reached, 1.0, 0.0), axis=0, keepdims=True)
            count = count + jnp.where(s1 == v1[a], extra, 0.0)
        e2_ref[hd] = jnp.exp(s2 - v2[0]).astype(e2_ref.dtype)
        r2_ref[hd] = rank2.astype(r2_ref.dtype)
        w_ref[hd] = jnp.exp(s1 - v1[0]) / z
        n_ref[hd] = count


def peer_select(h_t, w_scores, tm=512):
    d, t = h_t.shape
    tab = lambda dt: jax.ShapeDtypeStruct((PEER_HEADS, PEER_KEYS, t), dt)
    blk = pl.BlockSpec((PEER_HEADS, PEER_KEYS, tm), lambda i: (0, 0, i))
    return pl.pallas_call(
        _peer_select_kernel,
        grid=(t // tm,),
        in_specs=[pl.BlockSpec((d, tm), lambda i: (0, i)),
                  _resident(w_scores.shape)],
        out_specs=[blk, blk, blk, blk],
        out_shape=[tab(BF16), tab(BF16), tab(F32), tab(F32)],
        scratch_shapes=[pltpu.VMEM((CAND_ROWS, tm), F32), pltpu.VMEM((PEER_TOPK, tm), F32)],
        compiler_params=_params(1),
        name="peer_select",
    )(h_t, w_scores)


def _peer_dense_kernel(h_ref, u_ref, vt_ref, e2_ref, r2_ref, w_ref, n_ref, x_ref, g_ref,
                       *rest, first_keys, final):
    if final:
        out_ref, acc_ref = rest
    else:
        perm_ref, xo_ref, hn_ref, *hd_refs, acc_ref = rest
    e = pl.program_id(1)
    tm = h_ref.shape[1]

    @pl.when(e == 0)
    def _():
        acc_ref[...] = jnp.zeros_like(acc_ref)

    first_key = e * first_keys
    gates = []
    for ii in range(first_keys):
        g = None
        for hd in range(PEER_HEADS):
            n_row = jnp.broadcast_to(n_ref[hd, pl.ds(first_key + ii, 1), :].astype(BF16), (PEER_KEYS, tm))
            w_row = jnp.broadcast_to(w_ref[hd, pl.ds(first_key + ii, 1), :].astype(BF16), (PEER_KEYS, tm))
            picked = jnp.where(r2_ref[hd] < n_row, e2_ref[hd], jnp.zeros((), BF16)) * w_row
            g = picked if g is None else g + picked
        gates.append(g)
    gate_t = jnp.concatenate(gates, axis=0)

    a_t = jnp.dot(u_ref[...], h_ref[...], preferred_element_type=F32)
    act_t = _gelu(a_t).astype(BF16) * gate_t
    acc_ref[...] += jnp.dot(vt_ref[...], act_t, preferred_element_type=F32)

    @pl.when(e == pl.num_programs(1) - 1)
    def _():
        x = x_ref[...] + acc_ref[...].T
        if final:
            out_ref[...] = _rms(x, g_ref[...])
        else:
            xo_ref[...] = x
            _write_norms(x, g_ref[...], perm_ref, hn_ref, hd_refs)


def peer_dense(h_t, u, v_tiles, e2, r2, w, n, x, g_next, final, batch, seq, tm=512):
    d, t = h_t.shape
    tiles = seq // tm
    n_tiles, _, te = v_tiles.shape
    first_keys = te // PEER_KEYS
    tab = pl.BlockSpec((PEER_HEADS, PEER_KEYS, tm), lambda i, e: (0, 0, i))
    tok = pl.BlockSpec((tm, d), lambda i, e: (i, 0))
    operands = [h_t, u, v_tiles, e2, r2, w, n, x, g_next.reshape(1, d)]
    in_specs = [pl.BlockSpec((d, tm), lambda i, e: (0, i)),
                pl.BlockSpec((te, d), lambda i, e: (e, 0)),
                pl.BlockSpec((None, d, te), lambda i, e: (e, 0, 0)),
                tab, tab, tab, tab,
                pl.BlockSpec((tm, d), lambda i, e: (i, 0), pipeline_mode=pl.Buffered(1)),
                _resident((1, d))]
    once = pl.Buffered(1)
    tok_out = pl.BlockSpec((tm, d), lambda i, e: (i, 0), pipeline_mode=once)
    if final:
        out_specs = tok_out
        out_shape = jax.ShapeDtypeStruct((t, d), F32)
    else:
        specs, shapes = _norm_specs(batch, seq, d, tm, lambda i, e: (i // tiles, i % tiles), once)
        out_specs = [tok_out] + specs
        out_shape = [jax.ShapeDtypeStruct((t, d), F32)] + shapes
        perm = jnp.asarray(_group_by_residue_matrices(tm), BF16)
        operands.append(perm)
        in_specs.append(_resident(perm.shape))
    return pl.pallas_call(
        functools.partial(_peer_dense_kernel, first_keys=first_keys, final=final),
        grid=(t // tm, n_tiles),
        in_specs=in_specs,
        out_specs=out_specs,
        out_shape=out_shape,
        scratch_shapes=[pltpu.VMEM((d, tm), F32)],
        compiler_params=_params(2),
        name="peer_dense_final" if final else "peer_dense",
    )(*operands)


W_IN_GATE_COL = IN_WIDTH - GATE_WIDTH
W_IN_Q_COL = 2 * A_WIDTH + B_WIDTH


def _main_weight_block(j):
    gate_blocks = GATE_WIDTH // QKV_GROUP_WIDTH
    return jnp.where(j < gate_blocks, j + W_IN_GATE_COL // QKV_GROUP_WIDTH, j - gate_blocks)


def _qkv_weights(w, g):
    cols = [W_IN_Q_COL + part * C_QKV_WIDTH + g * C_OUT_WIDTH for part in range(3)]
    return jnp.concatenate([w[:, c:c + C_OUT_WIDTH] for c in cols], axis=1).astype(BF16)


def kernel(x, rel_bias_table, norm_mix_g, w_in, a_norm_g, a_w_s, a_b_s, b_w_pool, b_scale, w_br_a,
           w_br_b, w_br_c, w_out, norm_ffn_g, peer_w_q, peer_sub_keys_1, peer_sub_keys_2, peer_u,
           peer_v, final_norm_g):
    batch, seq, d = x.shape
    t = batch * seq
    xf = x.reshape(t, d)
    hs = rmsnorm_rows(xf, norm_mix_g[0], batch, seq)
    out = None
    for l in range(DEPTH):
        p = in_proj(hs[0], to_bf16(w_in, l, 128), MAIN_WIDTH, _main_weight_block)
        ya = mixer_a(p, a_norm_g[l], a_w_s[l], a_b_s[l])
        yb = mixer_b(p, b_w_pool[l].astype(BF16), b_scale[l], seq)
        outs, lses = [], []
        for g in range(C_GROUPS):
            qkv = in_proj(hs[g].reshape(t, d), _qkv_weights(w_in[l], g))
            o, lse = dilated_attention_group(qkv, 0, rel_bias_table, g, batch, seq)
            outs.append(o)
            lses.append(lse)
        x_mid, h_ffn = merge_and_project(
            xf, p, ya, yb, outs, lses, w_br_a[l].astype(BF16), w_br_b[l].astype(BF16),
            w_br_c[l].astype(BF16), w_out[l].astype(BF16), norm_ffn_g[l], batch, seq)
        w_scores = fold_sub_keys(peer_w_q, l, peer_sub_keys_1[l], peer_sub_keys_2[l])
        e2, r2, w, n = peer_select(h_ffn, w_scores)
        final = l == DEPTH - 1
        g_next = final_norm_g if final else norm_mix_g[l + 1]
        res = peer_dense(h_ffn, to_bf16(peer_u, l, PEER_EXPERT_TILE),
                         transposed_tiles_bf16(peer_v, l, PEER_EXPERT_TILE), e2, r2, w, n,
                         x_mid, g_next, final, batch, seq)
        if final:
            out = res
        else:
            xf, hs = res[0], res[1:]
    return out.reshape(batch, seq, d)
```

```python
import functools
import math

import numpy as np
import jax
import jax.numpy as jnp
from jax import lax
from jax.experimental import pallas as pl
from jax.experimental.pallas import tpu as pltpu

D_MODEL = 2048
DEPTH = 2
BLOCK = 128
EPS = 1e-6
A_HEADS = 8
A_HEAD_DIM = 128
A_WIDTH = A_HEADS * A_HEAD_DIM
POOL_WINDOWS = (2, 4, 8, 16)
B_GROUP_DIM = 256
B_WIDTH = len(POOL_WINDOWS) * B_GROUP_DIM
C_PATTERNS = ((128, 1), (512, 4), (2048, 16))
C_GROUPS = len(C_PATTERNS)
C_HEADS_PER_GROUP = 4
C_HEAD_DIM = 128
C_HEADS = C_GROUPS * C_HEADS_PER_GROUP
C_QKV_WIDTH = C_HEADS * C_HEAD_DIM
C_OUT_WIDTH = C_HEADS_PER_GROUP * C_HEAD_DIM
N_BRANCHES = 3
GATE_WIDTH = N_BRANCHES * D_MODEL
IN_WIDTH = 2 * A_WIDTH + B_WIDTH + 3 * C_QKV_WIDTH + GATE_WIDTH
REL_BUCKETS = 32
REL_MAX_DISTANCE = 2048
PEER_HEADS = 8
PEER_KEYS = 128
PEER_N_EXPERTS = PEER_KEYS * PEER_KEYS
PEER_KEY_DIM = 256
PEER_TOPK = 16
PEER_EXPERT_TILE = 1024

COL_GATE = 0
COL_AU = GATE_WIDTH
COL_AV = COL_AU + A_WIDTH
COL_B = COL_AV + A_WIDTH
MAIN_WIDTH = COL_B + B_WIDTH
QKV_GROUP_WIDTH = 3 * C_OUT_WIDTH

NEG_MASK = -1e30
BF16 = jnp.bfloat16
F32 = jnp.float32

VMEM_LIMIT_BYTES = 60 * 1024 * 1024


def _params(n_axes):
    return pltpu.CompilerParams(
        dimension_semantics=("arbitrary",) * n_axes, vmem_limit_bytes=VMEM_LIMIT_BYTES)


def _rms(x, g):
    return x * lax.rsqrt(jnp.mean(x * x, axis=-1, keepdims=True) + EPS) * g


def _gelu(x):
    return 0.5 * x * (1.0 + lax.erf(x * (1.0 / math.sqrt(2.0))))


def _resident(shape):
    nd = len(shape)
    return pl.BlockSpec(shape, lambda *_: (0,) * nd, pipeline_mode=pl.Buffered(1))


DILATIONS = tuple(d for _, d in C_PATTERNS if d > 1)


def _group_by_residue_matrices(tm):
    mats = []
    for dil in DILATIONS:
        dst = np.arange(tm)
        src = (dst % (tm // dil)) * dil + dst // (tm // dil)
        mats.append((src[:, None] == np.arange(tm)[None, :]).astype(np.float32))
    return np.stack(mats)


def _write_norms(x, g, perm_ref, h_ref, hd_refs):
    h = _rms(x, g).astype(h_ref.dtype)
    h_ref[...] = h
    for k, ref in enumerate(hd_refs):
        grouped = jnp.dot(perm_ref[k], h, preferred_element_type=F32).astype(ref.dtype)
        ref[...] = grouped.reshape(ref.shape)


def _norm_specs(batch, seq, d, tm, index, buffering=None):
    specs = [pl.BlockSpec((tm, d), lambda *ids: (index(*ids)[0] * (seq // tm) + index(*ids)[1], 0),
                          pipeline_mode=buffering)]
    shapes = [jax.ShapeDtypeStruct((batch * seq, d), BF16)]
    for dil in DILATIONS:
        specs.append(pl.BlockSpec((None, dil, tm // dil, d),
                                  lambda *ids: (index(*ids)[0], 0, index(*ids)[1], 0),
                                  pipeline_mode=buffering))
        shapes.append(jax.ShapeDtypeStruct((batch, dil, seq // dil, d), BF16))
    return specs, shapes


def _norm_kernel(x_ref, g_ref, perm_ref, h_ref, *hd_refs):
    _write_norms(x_ref[...], g_ref[...], perm_ref, h_ref, hd_refs)


def rmsnorm_rows(x, g, batch, seq, tm=512):
    t, d = x.shape
    tiles = seq // tm
    specs, shapes = _norm_specs(batch, seq, d, tm, lambda i: (i // tiles, i % tiles))
    perm = jnp.asarray(_group_by_residue_matrices(tm), BF16)
    return pl.pallas_call(
        _norm_kernel,
        grid=(t // tm,),
        in_specs=[pl.BlockSpec((tm, d), lambda i: (i, 0)), _resident((1, d)), _resident(perm.shape)],
        out_specs=specs,
        out_shape=shapes,
        compiler_params=_params(1),
        name="rmsnorm_rows",
    )(x, g.reshape(1, d), perm)


def _cast_kernel(x_ref, o_ref):
    o_ref[...] = x_ref[...].astype(o_ref.dtype)


def to_bf16(stack, layer, rows):
    _, n, d = stack.shape
    return pl.pallas_call(
        _cast_kernel,
        grid=(n // rows,),
        in_specs=[pl.BlockSpec((None, rows, d), lambda i: (layer, i, 0))],
        out_specs=pl.BlockSpec((rows, d), lambda i: (i, 0)),
        out_shape=jax.ShapeDtypeStruct((n, d), BF16),
        compiler_params=_params(1),
        name="to_bf16",
    )(stack)


def _transpose_cast_kernel(x_ref, o_ref):
    o_ref[...] = x_ref[...].T.astype(o_ref.dtype)


def transposed_tiles_bf16(stack, layer, rows):
    _, n, d = stack.shape
    return pl.pallas_call(
        _transpose_cast_kernel,
        grid=(n // rows,),
        in_specs=[pl.BlockSpec((None, rows, d), lambda i: (layer, i, 0))],
        out_specs=pl.BlockSpec((None, d, rows), lambda i: (i, 0, 0)),
        out_shape=jax.ShapeDtypeStruct((n // rows, d, rows), BF16),
        compiler_params=_params(1),
        name="transposed_tiles_bf16",
    )(stack)


def _matmul_kernel(h_ref, w_ref, o_ref):
    o_ref[...] = jnp.dot(h_ref[...], w_ref[...], preferred_element_type=F32).astype(o_ref.dtype)


def in_proj(h, w, n_out=None, w_block=lambda j: j, tm=1024, tn=QKV_GROUP_WIDTH):
    t, d = h.shape
    n = w.shape[1] if n_out is None else n_out
    return pl.pallas_call(
        _matmul_kernel,
        grid=(n // tn, t // tm),
        in_specs=[pl.BlockSpec((tm, d), lambda j, i: (i, 0)),
                  pl.BlockSpec((d, tn), lambda j, i: (0, w_block(j)))],
        out_specs=pl.BlockSpec((tm, tn), lambda j, i: (i, j)),
        out_shape=jax.ShapeDtypeStruct((t, n), BF16),
        compiler_params=_params(2),
        name="in_proj",
    )(h, w)


def _mixer_a_kernel(au_ref, av_ref, g_ref, ws_ref, bs_ref, o_ref, *, chunks):
    u = _gelu(au_ref[...].astype(F32))
    v = _rms(_gelu(av_ref[...].astype(F32)), g_ref[...]).astype(BF16)
    row = lax.broadcasted_iota(jnp.int32, (BLOCK, BLOCK), 0)
    col = lax.broadcasted_iota(jnp.int32, (BLOCK, BLOCK), 1)
    causal = col <= row
    for hd in range(A_HEADS):
        w = jnp.where(causal, ws_ref[hd], 0.0).astype(BF16)
        bias = bs_ref[hd]
        cs = slice(hd * A_HEAD_DIM, (hd + 1) * A_HEAD_DIM)
        for c in range(chunks):
            rs = slice(c * BLOCK, (c + 1) * BLOCK)
            s = jnp.dot(w, v[rs, cs], preferred_element_type=F32) + bias
            o_ref[rs, cs] = (u[rs, cs] * s).astype(o_ref.dtype)


def mixer_a(p, a_norm_g, a_w_s, a_b_s, chunks=4):
    t = p.shape[0]
    tm = chunks * BLOCK
    bias = jnp.broadcast_to(a_b_s[:, :, None], (A_HEADS, BLOCK, A_HEAD_DIM))
    return pl.pallas_call(
        functools.partial(_mixer_a_kernel, chunks=chunks),
        grid=(t // tm,),
        in_specs=[pl.BlockSpec((tm, A_WIDTH), lambda i: (i, COL_AU // A_WIDTH)),
                  pl.BlockSpec((tm, A_WIDTH), lambda i: (i, COL_AV // A_WIDTH)),
                  _resident((1, A_WIDTH)),
                  _resident((A_HEADS, BLOCK, BLOCK)),
                  _resident((A_HEADS, BLOCK, A_HEAD_DIM))],
        out_specs=pl.BlockSpec((tm, A_WIDTH), lambda i: (i, 0)),
        out_shape=jax.ShapeDtypeStruct((t, A_WIDTH), BF16),
        compiler_params=_params(1),
        name="mixer_a",
    )(p, p, a_norm_g.reshape(1, A_WIDTH), a_w_s, bias)


def _pool_band_matrices():
    t = np.arange(BLOCK)[:, None]
    k = np.arange(2 * BLOCK)[None, :]
    back = BLOCK + t - k
    return np.stack([((back >= 0) & (back < w)) for w in POOL_WINDOWS]).astype(np.float32)


def _mixer_b_kernel(prev_ref, cur_ref, band_ref, wp_ref, sc_ref, o_ref, *, chunks, tiles_per_seq):
    first = (pl.program_id(0) % tiles_per_seq) == 0
    prev = jnp.where(first, jnp.zeros_like(prev_ref[...]), prev_ref[...])
    tile_pos = (pl.program_id(0) % tiles_per_seq) * (chunks * BLOCK)
    for c in range(chunks):
        rs = slice(c * BLOCK, (c + 1) * BLOCK)
        before = prev if c == 0 else cur_ref[(c - 1) * BLOCK:c * BLOCK, :]
        cur = cur_ref[rs, :]
        both = jnp.concatenate([before, cur], axis=0)
        pos = tile_pos + c * BLOCK + lax.broadcasted_iota(jnp.int32, (BLOCK, B_GROUP_DIM), 0)
        for g, w in enumerate(POOL_WINDOWS):
            cs = slice(g * B_GROUP_DIM, (g + 1) * B_GROUP_DIM)
            wsum = jnp.dot(band_ref[g], both[:, cs], preferred_element_type=F32)
            count = jnp.minimum(pos + 1, w).astype(F32)
            diff = (wsum / count - cur[:, cs].astype(F32)).astype(BF16)
            y = jnp.dot(diff, wp_ref[g], preferred_element_type=F32) * sc_ref[:, cs]
            o_ref[rs, cs] = y.astype(o_ref.dtype)


def mixer_b(p, w_pool, scale, seq, chunks=4):
    t = p.shape[0]
    tm = chunks * BLOCK
    band = jnp.asarray(_pool_band_matrices(), BF16)
    col = COL_B // B_WIDTH
    return pl.pallas_call(
        functools.partial(_mixer_b_kernel, chunks=chunks, tiles_per_seq=seq // tm),
        grid=(t // tm,),
        in_specs=[pl.BlockSpec((BLOCK, B_WIDTH), lambda i: (jnp.maximum(i * chunks - 1, 0), col)),
                  pl.BlockSpec((tm, B_WIDTH), lambda i: (i, col)),
                  _resident((len(POOL_WINDOWS), BLOCK, 2 * BLOCK)),
                  _resident((len(POOL_WINDOWS), B_GROUP_DIM, B_GROUP_DIM)),
                  _resident((1, B_WIDTH))],
        out_specs=pl.BlockSpec((tm, B_WIDTH), lambda i: (i, 0)),
        out_shape=jax.ShapeDtypeStruct((t, B_WIDTH), BF16),
        compiler_params=_params(1),
        name="mixer_b",
    )(p, p, band, w_pool, scale.reshape(1, B_WIDTH))


def _rel_bucket(n):
    max_exact = REL_BUCKETS // 2
    n = np.asarray(n)
    nl = np.maximum(n, max_exact).astype(np.float64)
    large = max_exact + (np.log(nl / max_exact) / math.log(REL_MAX_DISTANCE / max_exact)
                         * (REL_BUCKETS - max_exact)).astype(np.int32)
    large = np.minimum(large, REL_BUCKETS - 1)
    return np.where(n < max_exact, n, large).astype(np.int32)


def _attn_bias(rel_table, g, dilation, n_off):
    qi = np.arange(BLOCK)[:, None]
    ki = np.arange(2 * BLOCK)[None, :]
    off = BLOCK + qi - ki
    ok = (off >= 0) & (off <= n_off)
    buckets = _rel_bucket(np.clip(off, 0, n_off) * dilation)
    onehot = (buckets.reshape(1, -1) == np.arange(REL_BUCKETS)[:, None]).astype(np.float32)
    heads = rel_table[:, g * C_HEADS_PER_GROUP:(g + 1) * C_HEADS_PER_GROUP]
    bias = jnp.dot(heads.T.astype(F32), jnp.asarray(onehot), precision=lax.Precision.HIGHEST)
    bias = bias.reshape(C_HEADS_PER_GROUP, BLOCK, 2 * BLOCK)
    return jnp.where(ok[None], bias, NEG_MASK)


ATTN_SUBS_PER_STEP = 8


def _attn_kernel(q_ref, kp_ref, kc_ref, vp_ref, vc_ref, bias_ref, o_ref, lse_ref):
    first = pl.program_id(1) == 0
    key_is_prev = lax.broadcasted_iota(jnp.int32, (BLOCK, 2 * BLOCK), 1) < BLOCK
    hide = jnp.logical_and(first, key_is_prev)
    scale = C_HEAD_DIM ** -0.5
    for r in range(q_ref.shape[0]):
        for hd in range(C_HEADS_PER_GROUP):
            cs = slice(hd * C_HEAD_DIM, (hd + 1) * C_HEAD_DIM)
            keys = jnp.concatenate([kp_ref[r, :, cs], kc_ref[r, :, cs]], axis=0)
            vals = jnp.concatenate([vp_ref[r, :, cs], vc_ref[r, :, cs]], axis=0)
            logits = lax.dot_general(q_ref[r, :, cs], keys, (((1,), (1,)), ((), ())),
                                     preferred_element_type=F32) * scale
            logits = jnp.where(hide, NEG_MASK, logits + bias_ref[hd])
            m = jnp.max(logits, axis=-1, keepdims=True)
            e = jnp.exp(logits - m)
            s = jnp.sum(e, axis=-1, keepdims=True)
            o = jnp.dot(e.astype(BF16), vals, preferred_element_type=F32) / s
            o_ref[r, :, cs] = o.astype(o_ref.dtype)
            lse_ref[r, :, cs] = jnp.broadcast_to(m + jnp.log(s), (BLOCK, C_HEAD_DIM))


def dilated_attention_group(qkv, first_col, rel_table, g, batch, seq):
    window, dilation = C_PATTERNS[g]
    n_off = window // dilation
    sub_len = seq // dilation
    nb = sub_len // BLOCK
    subs = batch * dilation
    pv = qkv.reshape(subs, sub_len, qkv.shape[1])
    bias = _attn_bias(rel_table, g, dilation, n_off)
    c0 = first_col // C_OUT_WIDTH

    def col(k):
        return lambda s, n: (s, n, c0 + k)

    def col_prev(k):
        return lambda s, n: (s, jnp.maximum(n - 1, 0), c0 + k)

    per_step = math.gcd(subs, ATTN_SUBS_PER_STEP)
    blk = (per_step, BLOCK, C_OUT_WIDTH)
    out_blk = pl.BlockSpec(blk, lambda s, n: (s, n, 0))
    o, lse = pl.pallas_call(
        _attn_kernel,
        grid=(subs // per_step, nb),
        in_specs=[pl.BlockSpec(blk, col(0)),
                  pl.BlockSpec(blk, col_prev(1)), pl.BlockSpec(blk, col(1)),
                  pl.BlockSpec(blk, col_prev(2)), pl.BlockSpec(blk, col(2)),
                  _resident((C_HEADS_PER_GROUP, BLOCK, 2 * BLOCK))],
        out_specs=[out_blk, out_blk],
        out_shape=[jax.ShapeDtypeStruct((subs, sub_len, C_OUT_WIDTH), BF16),
                   jax.ShapeDtypeStruct((subs, sub_len, C_OUT_WIDTH), F32)],
        compiler_params=_params(2),
        name=f"dilated_attention_{g}",
    )(pv, pv, pv, pv, pv, bias)
    return o, lse


def _merge_kernel(x_ref, gate_ref, ya_ref, yb_ref, o0_ref, o1_ref, o2_ref, l0_ref, l1_ref, l2_ref,
                  wa_ref, wb_ref, wc_ref, wo_ref, g_ref, perm_ref, xo_ref, h_ref):
    tm = x_ref.shape[0]

    def token_order(ref, k, precision=None):
        return jnp.dot(perm_ref[k].astype(ref.dtype), ref[...].reshape(tm, C_OUT_WIDTH),
                       preferred_element_type=F32, precision=precision)

    exact_f32 = lax.Precision.HIGHEST
    l0, l1, l2 = l0_ref[...], token_order(l1_ref, 0, exact_f32), token_order(l2_ref, 1, exact_f32)
    m = jnp.maximum(jnp.maximum(l0, l1), l2)
    e0, e1, e2 = jnp.exp(l0 - m), jnp.exp(l1 - m), jnp.exp(l2 - m)
    yc = (e0 * o0_ref[...].astype(F32) + e1 * token_order(o1_ref, 0)
          + e2 * token_order(o2_ref, 1)) / (e0 + e1 + e2)

    def branch(k, y, w_ref):
        gate = jax.nn.sigmoid(gate_ref[:, k * D_MODEL:(k + 1) * D_MODEL].astype(F32))
        return gate * jnp.dot(y, w_ref[...], preferred_element_type=F32)

    merged = (branch(0, ya_ref[...], wa_ref) + branch(1, yb_ref[...], wb_ref)
              + branch(2, yc.astype(BF16), wc_ref))
    x = x_ref[...] + jnp.dot(merged.astype(BF16), wo_ref[...], preferred_element_type=F32)
    xo_ref[...] = x
    h_ref[...] = _rms(x, g_ref[...]).T.astype(h_ref.dtype)


def merge_and_project(x, p, ya, yb, outs, lses, wa, wb, wc, wo, g_ffn, batch, seq, tm=256):
    t, d = x.shape
    tiles = seq // tm
    row = lambda w: pl.BlockSpec((tm, w), lambda i: (i, 0))

    def grouped(dil):
        return pl.BlockSpec((None, dil, tm // dil, C_OUT_WIDTH), lambda i: (i // tiles, 0, i % tiles, 0))

    def by_residue(a, dil):
        return a.reshape(batch, dil, seq // dil, C_OUT_WIDTH)

    d1, d2 = DILATIONS
    perm = jnp.asarray(_group_by_residue_matrices(tm).transpose(0, 2, 1), BF16)
    return pl.pallas_call(
        _merge_kernel,
        grid=(t // tm,),
        in_specs=[row(d), row(GATE_WIDTH), row(A_WIDTH), row(B_WIDTH),
                  row(C_OUT_WIDTH), grouped(d1), grouped(d2),
                  row(C_OUT_WIDTH), grouped(d1), grouped(d2),
                  _resident((A_WIDTH, d)), _resident((B_WIDTH, d)), _resident((C_OUT_WIDTH, d)),
                  _resident((d, d)), _resident((1, d)), _resident(perm.shape)],
        out_specs=[row(d), pl.BlockSpec((d, tm), lambda i: (0, i))],
        out_shape=[jax.ShapeDtypeStruct((t, d), F32), jax.ShapeDtypeStruct((d, t), BF16)],
        compiler_params=_params(1),
        name="merge_and_project",
    )(x, p, ya, yb,
      outs[0].reshape(t, C_OUT_WIDTH), by_residue(outs[1], d1), by_residue(outs[2], d2),
      lses[0].reshape(t, C_OUT_WIDTH), by_residue(lses[1], d1), by_residue(lses[2], d2),
      wa, wb, wc, wo, g_ffn.reshape(1, d), perm)


def _candidate_pairs():
    return [(a, b) for a in range(PEER_TOPK) for b in range(PEER_TOPK // (a + 1))]


CAND_ROWS = 64


SUBLANES = 8


def _sorting_network(n):
    pairs, p = [], 1
    while p < n:
        k = p
        while k >= 1:
            for j in range(k % p, n - k, 2 * k):
                for i in range(min(k, n - j - k)):
                    if (i + j) // (2 * p) == (i + j + k) // (2 * p):
                        pairs.append((i + j, i + j + k))
            k //= 2
        p *= 2
    return pairs


def _top16_desc(slabs):
    lst = list(slabs)
    for a, b in _sorting_network(len(lst)):
        lst[a], lst[b] = jnp.maximum(lst[a], lst[b]), jnp.minimum(lst[a], lst[b])
    out = []
    for k in range(PEER_TOPK):
        head = jnp.max(lst[0], axis=0, keepdims=True)
        out.append(head)
        if k == PEER_TOPK - 1:
            break
        taken = lst[0] == head
        keep = min(len(lst), PEER_TOPK - k - 1)
        lst = [jnp.where(taken, lst[a + 1] if a + 1 < len(lst) else -jnp.inf, lst[a])
               for a in range(keep)]
    return out


def _row_slabs(x):
    return [x[r:r + SUBLANES, :] for r in range(0, x.shape[0], SUBLANES)]


COUNT_SPLIT_B = 4
COUNT_SPLIT_A = 3
assert (COUNT_SPLIT_A + 1) * (COUNT_SPLIT_B + 1) > PEER_TOPK


def _fold_keys_kernel(k_ref, wq_ref, o_ref):
    o_ref[...] = lax.dot_general(k_ref[...], wq_ref[...], (((1,), (1,)), ((), ())),
                                 preferred_element_type=F32,
                                 precision=lax.Precision.HIGHEST).astype(o_ref.dtype)


def fold_sub_keys(w_q, layer, k1, k2):
    _, d, n = w_q.shape
    half = PEER_KEY_DIM // 2
    keys = jnp.stack([k1, k2])
    return pl.pallas_call(
        _fold_keys_kernel,
        grid=(n // half,),
        in_specs=[pl.BlockSpec((None, PEER_KEYS, half), lambda i: (i % 2, 0, 0)),
                  pl.BlockSpec((None, d, half), lambda i: (layer, 0, i))],
        out_specs=pl.BlockSpec((PEER_KEYS, d), lambda i: (i, 0)),
        out_shape=jax.ShapeDtypeStruct((n // half * PEER_KEYS, d), BF16),
        compiler_params=_params(1),
        name="fold_sub_keys",
    )(keys, w_q)


def _peer_select_kernel(h_ref, ws_ref, e2_ref, r2_ref, w_ref, n_ref, cand_ref, v2_ref):
    tm = h_ref.shape[1]
    scores = jnp.dot(ws_ref[...], h_ref[...], preferred_element_type=F32)
    cand_ref[...] = jnp.full(cand_ref.shape, -jnp.inf, F32)
    for hd in range(PEER_HEADS):
        s1 = scores[(2 * hd) * PEER_KEYS:(2 * hd + 1) * PEER_KEYS, :]
        s2 = scores[(2 * hd + 1) * PEER_KEYS:(2 * hd + 2) * PEER_KEYS, :]
        v1 = _top16_desc(_row_slabs(s1))
        v2 = _top16_desc(_row_slabs(s2))
        for b in range(PEER_TOPK):
            v2_ref[b:b + 1, :] = v2[b]
        for idx, (a, b) in enumerate(_candidate_pairs()):
            cand_ref[idx:idx + 1, :] = v1[a] + v2[b]
        best = _top16_desc(_row_slabs(cand_ref[...]))
        tau = best[PEER_TOPK - 1]
        z = functools.reduce(lambda acc, c: acc + jnp.exp(c - best[0]), best[1:],
                             jnp.ones_like(best[0]))
        rank2 = jnp.zeros(s2.shape, F32)
        for b in range(PEER_TOPK):
            rank2 = jnp.where(v2[b] > s2, float(b + 1), rank2)
        count = jnp.zeros(s1.shape, F32)
        for b in range(COUNT_SPLIT_B):
            count = jnp.where(s1 + v2[b] >= tau, float(b + 1), count)
        v2_all = v2_ref[...]
        low_rank = lax.broadcasted_iota(jnp.int32, v2_all.shape, 0) >= COUNT_SPLIT_B
        for a in range(COUNT_SPLIT_A):
            reached = jnp.logical_and(v1[a] + v2_all >= tau, low_rank)
            extra = jnp.sum(jnp.where(reached, 1.0, 0.0), axis=0, keepdims=True)
            count = count + jnp.where(s1 == v1[a], extra, 0.0)
        e2_ref[hd] = jnp.exp(s2 - v2[0]).astype(e2_ref.dtype)
        r2_ref[hd] = rank2.astype(r2_ref.dtype)
        w_ref[hd] = jnp.exp(s1 - v1[0]) / z
        n_ref[hd] = count


def peer_select(h_t, w_scores, tm=512):
    d, t = h_t.shape
    tab = lambda dt: jax.ShapeDtypeStruct((PEER_HEADS, PEER_KEYS, t), dt)
    blk = pl.BlockSpec((PEER_HEADS, PEER_KEYS, tm), lambda i: (0, 0, i))
    return pl.pallas_call(
        _peer_select_kernel,
        grid=(t // tm,),
        in_specs=[pl.BlockSpec((d, tm), lambda i: (0, i)),
                  _resident(w_scores.shape)],
        out_specs=[blk, blk, blk, blk],
        out_shape=[tab(BF16), tab(BF16), tab(F32), tab(F32)],
        scratch_shapes=[pltpu.VMEM((CAND_ROWS, tm), F32), pltpu.VMEM((PEER_TOPK, tm), F32)],
        compiler_params=_params(1),
        name="peer_select",
    )(h_t, w_scores)


def _peer_dense_kernel(h_ref, u_ref, vt_ref, e2_ref, r2_ref, w_ref, n_ref, x_ref, g_ref,
                       *rest, first_keys, final):
    if final:
        out_ref, acc_ref = rest
    else:
        perm_ref, xo_ref, hn_ref, *hd_refs, acc_ref = rest
    e = pl.program_id(1)
    tm = h_ref.shape[1]

    @pl.when(e == 0)
    def _():
        acc_ref[...] = jnp.zeros_like(acc_ref)

    first_key = e * first_keys
    gates = []
    for ii in range(first_keys):
        g = None
        for hd in range(PEER_HEADS):
            n_row = jnp.broadcast_to(n_ref[hd, pl.ds(first_key + ii, 1), :].astype(BF16), (PEER_KEYS, tm))
            w_row = jnp.broadcast_to(w_ref[hd, pl.ds(first_key + ii, 1), :].astype(BF16), (PEER_KEYS, tm))
            picked = jnp.where(r2_ref[hd] < n_row, e2_ref[hd], jnp.zeros((), BF16)) * w_row
            g = picked if g is None else g + picked
        gates.append(g)
    gate_t = jnp.concatenate(gates, axis=0)

    a_t = jnp.dot(u_ref[...], h_ref[...], preferred_element_type=F32)
    act_t = _gelu(a_t).astype(BF16) * gate_t
    acc_ref[...] += jnp.dot(vt_ref[...], act_t, preferred_element_type=F32)

    @pl.when(e == pl.num_programs(1) - 1)
    def _():
        x = x_ref[...] + acc_ref[...].T
        if final:
            out_ref[...] = _rms(x, g_ref[...])
        else:
            xo_ref[...] = x
            _write_norms(x, g_ref[...], perm_ref, hn_ref, hd_refs)


def peer_dense(h_t, u, v_tiles, e2, r2, w, n, x, g_next, final, batch, seq, tm=512):
    d, t = h_t.shape
    tiles = seq // tm
    n_tiles, _, te = v_tiles.shape
    first_keys = te // PEER_KEYS
    tab = pl.BlockSpec((PEER_HEADS, PEER_KEYS, tm), lambda i, e: (0, 0, i))
    operands = [h_t, u, v_tiles, e2, r2, w, n, x, g_next.reshape(1, d)]
    in_specs = [pl.BlockSpec((d, tm), lambda i, e: (0, i)),
                pl.BlockSpec((te, d), lambda i, e: (e, 0)),
                pl.BlockSpec((None, d, te), lambda i, e: (e, 0, 0)),
                tab, tab, tab, tab,
                pl.BlockSpec((tm, d), lambda i, e: (i, 0), pipeline_mode=pl.Buffered(1)),
                _resident((1, d))]
    once = pl.Buffered(1)
    tok_out = pl.BlockSpec((tm, d), lambda i, e: (i, 0), pipeline_mode=once)
    if final:
        out_specs = tok_out
        out_shape = jax.ShapeDtypeStruct((t, d), F32)
    else:
        specs, shapes = _norm_specs(batch, seq, d, tm, lambda i, e: (i // tiles, i % tiles), once)
        out_specs = [tok_out] + specs
        out_shape = [jax.ShapeDtypeStruct((t, d), F32)] + shapes
        perm = jnp.asarray(_group_by_residue_matrices(tm), BF16)
        operands.append(perm)
        in_specs.append(_resident(perm.shape))
    return pl.pallas_call(
        functools.partial(_peer_dense_kernel, first_keys=first_keys, final=final),
        grid=(t // tm, n_tiles),
        in_specs=in_specs,
        out_specs=out_specs,
        out_shape=out_shape,
        scratch_shapes=[pltpu.VMEM((d, tm), F32)],
        compiler_params=_params(2),
        name="peer_dense_final" if final else "peer_dense",
    )(*operands)


W_IN_GATE_COL = IN_WIDTH - GATE_WIDTH
W_IN_Q_COL = 2 * A_WIDTH + B_WIDTH


def _main_weight_block(j):
    gate_blocks = GATE_WIDTH // QKV_GROUP_WIDTH
    return jnp.where(j < gate_blocks, j + W_IN_GATE_COL // QKV_GROUP_WIDTH, j - gate_blocks)


def _qkv_weights(w, g):
    cols = [W_IN_Q_COL + part * C_QKV_WIDTH + g * C_OUT_WIDTH for part in range(3)]
    return jnp.concatenate([w[:, c:c + C_OUT_WIDTH] for c in cols], axis=1).astype(BF16)


def kernel(x, rel_bias_table, norm_mix_g, w_in, a_norm_g, a_w_s, a_b_s, b_w_pool, b_scale, w_br_a,
           w_br_b, w_br_c, w_out, norm_ffn_g, peer_w_q, peer_sub_keys_1, peer_sub_keys_2, peer_u,
           peer_v, final_norm_g):
    batch, seq, d = x.shape
    t = batch * seq
    xf = x.reshape(t, d)
    hs = rmsnorm_rows(xf, norm_mix_g[0], batch, seq)
    out = None
    for l in range(DEPTH):
        p = in_proj(hs[0], to_bf16(w_in, l, 128), MAIN_WIDTH, _main_weight_block)
        ya = mixer_a(p, a_norm_g[l], a_w_s[l], a_b_s[l])
        yb = mixer_b(p, b_w_pool[l].astype(BF16), b_scale[l], seq)
        outs, lses = [], []
        for g in range(C_GROUPS):
            qkv = in_proj(hs[g].reshape(t, d), _qkv_weights(w_in[l], g))
            o, lse = dilated_attention_group(qkv, 0, rel_bias_table, g, batch, seq)
            outs.append(o)
            lses.append(lse)
        x_mid, h_ffn = merge_and_project(
            xf, p, ya, yb, outs, lses, w_br_a[l].astype(BF16), w_br_b[l].astype(BF16),
            w_br_c[l].astype(BF16), w_out[l].astype(BF16), norm_ffn_g[l], batch, seq)
        w_scores = fold_sub_keys(peer_w_q, l, peer_sub_keys_1[l], peer_sub_keys_2[l])
        e2, r2, w, n = peer_select(h_ffn, w_scores)
        final = l == DEPTH - 1
        g_next = final_norm_g if final else norm_mix_g[l + 1]
        res = peer_dense(h_ffn, to_bf16(peer_u, l, PEER_EXPERT_TILE),
                         transposed_tiles_bf16(peer_v, l, PEER_EXPERT_TILE), e2, r2, w, n,
                         x_mid, g_next, final, batch, seq)
        if final:
            out = res
        else:
            xf, hs = res[0], res[1:]
    return out.reshape(batch, seq, d)
```

```python
import functools
import math

import numpy as np
import jax
import jax.numpy as jnp
from jax import lax
from jax.experimental import pallas as pl
from jax.experimental.pallas import tpu as pltpu

D_MODEL = 2048
DEPTH = 2
BLOCK = 128
EPS = 1e-6
A_HEADS = 8
A_HEAD_DIM = 128
A_WIDTH = A_HEADS * A_HEAD_DIM
POOL_WINDOWS = (2, 4, 8, 16)
B_GROUP_DIM = 256
B_WIDTH = len(POOL_WINDOWS) * B_GROUP_DIM
C_PATTERNS = ((128, 1), (512, 4), (2048, 16))
C_GROUPS = len(C_PATTERNS)
C_HEADS_PER_GROUP = 4
C_HEAD_DIM = 128
C_HEADS = C_GROUPS * C_HEADS_PER_GROUP
C_QKV_WIDTH = C_HEADS * C_HEAD_DIM
C_OUT_WIDTH = C_HEADS_PER_GROUP * C_HEAD_DIM
N_BRANCHES = 3
GATE_WIDTH = N_BRANCHES * D_MODEL
IN_WIDTH = 2 * A_WIDTH + B_WIDTH + 3 * C_QKV_WIDTH + GATE_WIDTH
REL_BUCKETS = 32
REL_MAX_DISTANCE = 2048
PEER_HEADS = 8
PEER_KEYS = 128
PEER_N_EXPERTS = PEER_KEYS * PEER_KEYS
PEER_KEY_DIM = 256
PEER_TOPK = 16
PEER_EXPERT_TILE = 1024

COL_GATE = 0
COL_AU = GATE_WIDTH
COL_AV = COL_AU + A_WIDTH
COL_B = COL_AV + A_WIDTH
MAIN_WIDTH = COL_B + B_WIDTH
QKV_GROUP_WIDTH = 3 * C_OUT_WIDTH

NEG_MASK = -1e30
BF16 = jnp.bfloat16
F32 = jnp.float32

VMEM_LIMIT_BYTES = 60 * 1024 * 1024


def _params(n_axes):
    return pltpu.CompilerParams(
        dimension_semantics=("arbitrary",) * n_axes, vmem_limit_bytes=VMEM_LIMIT_BYTES)


def _rms(x, g):
    return x * lax.rsqrt(jnp.mean(x * x, axis=-1, keepdims=True) + EPS) * g


def _gelu(x):
    return 0.5 * x * (1.0 + lax.erf(x * (1.0 / math.sqrt(2.0))))


def _resident(shape):
    nd = len(shape)
    return pl.BlockSpec(shape, lambda *_: (0,) * nd, pipeline_mode=pl.Buffered(1))


DILATIONS = tuple(d for _, d in C_PATTERNS if d > 1)


def _group_by_residue_matrices(tm):
    mats = []
    for dil in DILATIONS:
        dst = np.arange(tm)
        src = (dst % (tm // dil)) * dil + dst // (tm // dil)
        mats.append((src[:, None] == np.arange(tm)[None, :]).astype(np.float32))
    return np.stack(mats)


def _write_norms(x, g, perm_ref, h_ref, hd_refs):
    h = _rms(x, g).astype(h_ref.dtype)
    h_ref[...] = h
    for k, ref in enumerate(hd_refs):
        grouped = jnp.dot(perm_ref[k], h, preferred_element_type=F32).astype(ref.dtype)
        ref[...] = grouped.reshape(ref.shape)


def _norm_specs(batch, seq, d, tm, index, buffering=None):
    specs = [pl.BlockSpec((tm, d), lambda *ids: (index(*ids)[0] * (seq // tm) + index(*ids)[1], 0),
                          pipeline_mode=buffering)]
    shapes = [jax.ShapeDtypeStruct((batch * seq, d), BF16)]
    for dil in DILATIONS:
        specs.append(pl.BlockSpec((None, dil, tm // dil, d),
                                  lambda *ids: (index(*ids)[0], 0, index(*ids)[1], 0),
                                  pipeline_mode=buffering))
        shapes.append(jax.ShapeDtypeStruct((batch, dil, seq // dil, d), BF16))
    return specs, shapes


def _norm_kernel(x_ref, g_ref, perm_ref, h_ref, *hd_refs):
    _write_norms(x_ref[...], g_ref[...], perm_ref, h_ref, hd_refs)


def rmsnorm_rows(x, g, batch, seq, tm=512):
    t, d = x.shape
    tiles = seq // tm
    specs, shapes = _norm_specs(batch, seq, d, tm, lambda i: (i // tiles, i % tiles))
    perm = jnp.asarray(_group_by_residue_matrices(tm), BF16)
    return pl.pallas_call(
        _norm_kernel,
        grid=(t // tm,),
        in_specs=[pl.BlockSpec((tm, d), lambda i: (i, 0)), _resident((1, d)), _resident(perm.shape)],
        out_specs=specs,
        out_shape=shapes,
        compiler_params=_params(1),
        name="rmsnorm_rows",
    )(x, g.reshape(1, d), perm)


def _cast_kernel(x_ref, o_ref):
    o_ref[...] = x_ref[...].astype(o_ref.dtype)


def to_bf16(stack, layer, rows):
    _, n, d = stack.shape
    return pl.pallas_call(
        _cast_kernel,
        grid=(n // rows,),
        in_specs=[pl.BlockSpec((None, rows, d), lambda i: (layer, i, 0))],
        out_specs=pl.BlockSpec((rows, d), lambda i: (i, 0)),
        out_shape=jax.ShapeDtypeStruct((n, d), BF16),
        compiler_params=_params(1),
        name="to_bf16",
    )(stack)


def _transpose_cast_kernel(x_ref, o_ref):
    o_ref[...] = x_ref[...].T.astype(o_ref.dtype)


def transposed_tiles_bf16(stack, layer, rows):
    _, n, d = stack.shape
    return pl.pallas_call(
        _transpose_cast_kernel,
        grid=(n // rows,),
        in_specs=[pl.BlockSpec((None, rows, d), lambda i: (layer, i, 0))],
        out_specs=pl.BlockSpec((None, d, rows), lambda i: (i, 0, 0)),
        out_shape=jax.ShapeDtypeStruct((n // rows, d, rows), BF16),
        compiler_params=_params(1),
        name="transposed_tiles_bf16",
    )(stack)


def _matmul_kernel(h_ref, w_ref, o_ref):
    o_ref[...] = jnp.dot(h_ref[...], w_ref[...], preferred_element_type=F32).astype(o_ref.dtype)


def in_proj(h, w, n_out=None, w_block=lambda j: j, tm=1024, tn=QKV_GROUP_WIDTH):
    t, d = h.shape
    n = w.shape[1] if n_out is None else n_out
    return pl.pallas_call(
        _matmul_kernel,
        grid=(n // tn, t // tm),
        in_specs=[pl.BlockSpec((tm, d), lambda j, i: (i, 0)),
                  pl.BlockSpec((d, tn), lambda j, i: (0, w_block(j)))],
        out_specs=pl.BlockSpec((tm, tn), lambda j, i: (i, j)),
        out_shape=jax.ShapeDtypeStruct((t, n), BF16),
        compiler_params=_params(2),
        name="in_proj",
    )(h, w)


def _matmul_f32_weight_kernel(h_ref, w_ref, o_ref, w_bf16):
    @pl.when(pl.program_id(1) == 0)
    def _():
        w_bf16[...] = w_ref[...].astype(w_bf16.dtype)

    o_ref[...] = jnp.dot(h_ref[...], w_bf16[...], preferred_element_type=F32).astype(o_ref.dtype)


def in_proj_f32_weights(h, w_stack, layer, n_out, w_block, tm=1024, tn=QKV_GROUP_WIDTH):
    t, d = h.shape
    return pl.pallas_call(
        _matmul_f32_weight_kernel,
        grid=(n_out // tn, t // tm),
        in_specs=[pl.BlockSpec((tm, d), lambda j, i: (i, 0)),
                  pl.BlockSpec((None, d, tn), lambda j, i: (layer, 0, w_block(j)))],
        out_specs=pl.BlockSpec((tm, tn), lambda j, i: (i, j)),
        out_shape=jax.ShapeDtypeStruct((t, n_out), BF16),
        scratch_shapes=[pltpu.VMEM((d, tn), BF16)],
        compiler_params=_params(2),
        name="in_proj_main",
    )(h, w_stack)


def _mixer_a_kernel(au_ref, av_ref, g_ref, ws_ref, bs_ref, o_ref, *, chunks):
    u = _gelu(au_ref[...].astype(F32))
    v = _rms(_gelu(av_ref[...].astype(F32)), g_ref[...]).astype(BF16)
    row = lax.broadcasted_iota(jnp.int32, (BLOCK, BLOCK), 0)
    col = lax.broadcasted_iota(jnp.int32, (BLOCK, BLOCK), 1)
    causal = col <= row
    for hd in range(A_HEADS):
        w = jnp.where(causal, ws_ref[hd], 0.0).astype(BF16)
        bias = bs_ref[hd]
        cs = slice(hd * A_HEAD_DIM, (hd + 1) * A_HEAD_DIM)
        for c in range(chunks):
            rs = slice(c * BLOCK, (c + 1) * BLOCK)
            s = jnp.dot(w, v[rs, cs], preferred_element_type=F32) + bias
            o_ref[rs, cs] = (u[rs, cs] * s).astype(o_ref.dtype)


def mixer_a(p, a_norm_g, a_w_s, a_b_s, chunks=4):
    t = p.shape[0]
    tm = chunks * BLOCK
    bias = jnp.broadcast_to(a_b_s[:, :, None], (A_HEADS, BLOCK, A_HEAD_DIM))
    return pl.pallas_call(
        functools.partial(_mixer_a_kernel, chunks=chunks),
        grid=(t // tm,),
        in_specs=[pl.BlockSpec((tm, A_WIDTH), lambda i: (i, COL_AU // A_WIDTH)),
                  pl.BlockSpec((tm, A_WIDTH), lambda i: (i, COL_AV // A_WIDTH)),
                  _resident((1, A_WIDTH)),
                  _resident((A_HEADS, BLOCK, BLOCK)),
                  _resident((A_HEADS, BLOCK, A_HEAD_DIM))],
        out_specs=pl.BlockSpec((tm, A_WIDTH), lambda i: (i, 0)),
        out_shape=jax.ShapeDtypeStruct((t, A_WIDTH), BF16),
        compiler_params=_params(1),
        name="mixer_a",
    )(p, p, a_norm_g.reshape(1, A_WIDTH), a_w_s, bias)


def _pool_band_matrices():
    t = np.arange(BLOCK)[:, None]
    k = np.arange(2 * BLOCK)[None, :]
    back = BLOCK + t - k
    return np.stack([((back >= 0) & (back < w)) for w in POOL_WINDOWS]).astype(np.float32)


def _mixer_b_kernel(prev_ref, cur_ref, band_ref, wp_ref, sc_ref, o_ref, *, chunks, tiles_per_seq):
    first = (pl.program_id(0) % tiles_per_seq) == 0
    prev = jnp.where(first, jnp.zeros_like(prev_ref[...]), prev_ref[...])
    tile_pos = (pl.program_id(0) % tiles_per_seq) * (chunks * BLOCK)
    for c in range(chunks):
        rs = slice(c * BLOCK, (c + 1) * BLOCK)
        before = prev if c == 0 else cur_ref[(c - 1) * BLOCK:c * BLOCK, :]
        cur = cur_ref[rs, :]
        both = jnp.concatenate([before, cur], axis=0)
        pos = tile_pos + c * BLOCK + lax.broadcasted_iota(jnp.int32, (BLOCK, B_GROUP_DIM), 0)
        for g, w in enumerate(POOL_WINDOWS):
            cs = slice(g * B_GROUP_DIM, (g + 1) * B_GROUP_DIM)
            wsum = jnp.dot(band_ref[g], both[:, cs], preferred_element_type=F32)
            count = jnp.minimum(pos + 1, w).astype(F32)
            diff = (wsum / count - cur[:, cs].astype(F32)).astype(BF16)
            y = jnp.dot(diff, wp_ref[g], preferred_element_type=F32) * sc_ref[:, cs]
            o_ref[rs, cs] = y.astype(o_ref.dtype)


def mixer_b(p, w_pool, scale, seq, chunks=4):
    t = p.shape[0]
    tm = chunks * BLOCK
    band = jnp.asarray(_pool_band_matrices(), BF16)
    col = COL_B // B_WIDTH
    return pl.pallas_call(
        functools.partial(_mixer_b_kernel, chunks=chunks, tiles_per_seq=seq // tm),
        grid=(t // tm,),
        in_specs=[pl.BlockSpec((BLOCK, B_WIDTH), lambda i: (jnp.maximum(i * chunks - 1, 0), col)),
                  pl.BlockSpec((tm, B_WIDTH), lambda i: (i, col)),
                  _resident((len(POOL_WINDOWS), BLOCK, 2 * BLOCK)),
                  _resident((len(POOL_WINDOWS), B_GROUP_DIM, B_GROUP_DIM)),
                  _resident((1, B_WIDTH))],
        out_specs=pl.BlockSpec((tm, B_WIDTH), lambda i: (i, 0)),
        out_shape=jax.ShapeDtypeStruct((t, B_WIDTH), BF16),
        compiler_params=_params(1),
        name="mixer_b",
    )(p, p, band, w_pool, scale.reshape(1, B_WIDTH))


def _rel_bucket(n):
    max_exact = REL_BUCKETS // 2
    n = np.asarray(n)
    nl = np.maximum(n, max_exact).astype(np.float64)
    large = max_exact + (np.log(nl / max_exact) / math.log(REL_MAX_DISTANCE / max_exact)
                         * (REL_BUCKETS - max_exact)).astype(np.int32)
    large = np.minimum(large, REL_BUCKETS - 1)
    return np.where(n < max_exact, n, large).astype(np.int32)


def _attn_bias(rel_table, g, dilation, n_off):
    qi = np.arange(BLOCK)[:, None]
    ki = np.arange(2 * BLOCK)[None, :]
    off = BLOCK + qi - ki
    ok = (off >= 0) & (off <= n_off)
    buckets = _rel_bucket(np.clip(off, 0, n_off) * dilation)
    onehot = (buckets.reshape(1, -1) == np.arange(REL_BUCKETS)[:, None]).astype(np.float32)
    heads = rel_table[:, g * C_HEADS_PER_GROUP:(g + 1) * C_HEADS_PER_GROUP]
    bias = jnp.dot(heads.T.astype(F32), jnp.asarray(onehot), precision=lax.Precision.HIGHEST)
    bias = bias.reshape(C_HEADS_PER_GROUP, BLOCK, 2 * BLOCK)
    return jnp.where(ok[None], bias, NEG_MASK)


ATTN_SUBS_PER_STEP = 8


def _attn_kernel(q_ref, kp_ref, kc_ref, vp_ref, vc_ref, bias_ref, o_ref, lse_ref):
    first = pl.program_id(1) == 0
    key_is_prev = lax.broadcasted_iota(jnp.int32, (BLOCK, 2 * BLOCK), 1) < BLOCK
    hide = jnp.logical_and(first, key_is_prev)
    scale = C_HEAD_DIM ** -0.5
    for r in range(q_ref.shape[0]):
        for hd in range(C_HEADS_PER_GROUP):
            cs = slice(hd * C_HEAD_DIM, (hd + 1) * C_HEAD_DIM)
            keys = jnp.concatenate([kp_ref[r, :, cs], kc_ref[r, :, cs]], axis=0)
            vals = jnp.concatenate([vp_ref[r, :, cs], vc_ref[r, :, cs]], axis=0)
            logits = lax.dot_general(q_ref[r, :, cs], keys, (((1,), (1,)), ((), ())),
                                     preferred_element_type=F32) * scale
            logits = jnp.where(hide, NEG_MASK, logits + bias_ref[hd])
            m = jnp.max(logits, axis=-1, keepdims=True)
            e = jnp.exp(logits - m)
            s = jnp.sum(e, axis=-1, keepdims=True)
            o = jnp.dot(e.astype(BF16), vals, preferred_element_type=F32) / s
            o_ref[r, :, cs] = o.astype(o_ref.dtype)
            lse_ref[r, :, cs] = jnp.broadcast_to(m + jnp.log(s), (BLOCK, C_HEAD_DIM))


def dilated_attention_group(qkv, first_col, rel_table, g, batch, seq):
    window, dilation = C_PATTERNS[g]
    n_off = window // dilation
    sub_len = seq // dilation
    nb = sub_len // BLOCK
    subs = batch * dilation
    pv = qkv.reshape(subs, sub_len, qkv.shape[1])
    bias = _attn_bias(rel_table, g, dilation, n_off)
    c0 = first_col // C_OUT_WIDTH

    def col(k):
        return lambda s, n: (s, n, c0 + k)

    def col_prev(k):
        return lambda s, n: (s, jnp.maximum(n - 1, 0), c0 + k)

    per_step = math.gcd(subs, ATTN_SUBS_PER_STEP)
    blk = (per_step, BLOCK, C_OUT_WIDTH)
    out_blk = pl.BlockSpec(blk, lambda s, n: (s, n, 0))
    o, lse = pl.pallas_call(
        _attn_kernel,
        grid=(subs // per_step, nb),
        in_specs=[pl.BlockSpec(blk, col(0)),
                  pl.BlockSpec(blk, col_prev(1)), pl.BlockSpec(blk, col(1)),
                  pl.BlockSpec(blk, col_prev(2)), pl.BlockSpec(blk, col(2)),
                  _resident((C_HEADS_PER_GROUP, BLOCK, 2 * BLOCK))],
        out_specs=[out_blk, out_blk],
        out_shape=[jax.ShapeDtypeStruct((subs, sub_len, C_OUT_WIDTH), BF16),
                   jax.ShapeDtypeStruct((subs, sub_len, C_OUT_WIDTH), F32)],
        compiler_params=_params(2),
        name=f"dilated_attention_{g}",
    )(pv, pv, pv, pv, pv, bias)
    return o, lse


def _merge_kernel(x_ref, gate_ref, ya_ref, yb_ref, o0_ref, o1_ref, o2_ref, l0_ref, l1_ref, l2_ref,
                  wa_ref, wb_ref, wc_ref, wo_ref, g_ref, perm_ref, xo_ref, h_ref):
    tm = x_ref.shape[0]

    def token_order(ref, k, precision=None):
        return jnp.dot(perm_ref[k].astype(ref.dtype), ref[...].reshape(tm, C_OUT_WIDTH),
                       preferred_element_type=F32, precision=precision)

    exact_f32 = lax.Precision.HIGHEST
    l0, l1, l2 = l0_ref[...], token_order(l1_ref, 0, exact_f32), token_order(l2_ref, 1, exact_f32)
    m = jnp.maximum(jnp.maximum(l0, l1), l2)
    e0, e1, e2 = jnp.exp(l0 - m), jnp.exp(l1 - m), jnp.exp(l2 - m)
    yc = (e0 * o0_ref[...].astype(F32) + e1 * token_order(o1_ref, 0)
          + e2 * token_order(o2_ref, 1)) / (e0 + e1 + e2)

    def branch(k, y, w_ref):
        gate = jax.nn.sigmoid(gate_ref[:, k * D_MODEL:(k + 1) * D_MODEL].astype(F32))
        return gate * jnp.dot(y, w_ref[...], preferred_element_type=F32)

    merged = (branch(0, ya_ref[...], wa_ref) + branch(1, yb_ref[...], wb_ref)
              + branch(2, yc.astype(BF16), wc_ref))
    x = x_ref[...] + jnp.dot(merged.astype(BF16), wo_ref[...], preferred_element_type=F32)
    xo_ref[...] = x
    h_ref[...] = _rms(x, g_ref[...]).T.astype(h_ref.dtype)


def merge_and_project(x, p, ya, yb, outs, lses, wa, wb, wc, wo, g_ffn, batch, seq, tm=256):
    t, d = x.shape
    tiles = seq // tm
    row = lambda w: pl.BlockSpec((tm, w), lambda i: (i, 0))

    def grouped(dil):
        return pl.BlockSpec((None, dil, tm // dil, C_OUT_WIDTH), lambda i: (i // tiles, 0, i % tiles, 0))

    def by_residue(a, dil):
        return a.reshape(batch, dil, seq // dil, C_OUT_WIDTH)

    d1, d2 = DILATIONS
    perm = jnp.asarray(_group_by_residue_matrices(tm).transpose(0, 2, 1), BF16)
    return pl.pallas_call(
        _merge_kernel,
        grid=(t // tm,),
        in_specs=[row(d), row(GATE_WIDTH), row(A_WIDTH), row(B_WIDTH),
                  row(C_OUT_WIDTH), grouped(d1), grouped(d2),
                  row(C_OUT_WIDTH), grouped(d1), grouped(d2),
                  _resident((A_WIDTH, d)), _resident((B_WIDTH, d)), _resident((C_OUT_WIDTH, d)),
                  _resident((d, d)), _resident((1, d)), _resident(perm.shape)],
        out_specs=[row(d), pl.BlockSpec((d, tm), lambda i: (0, i))],
        out_shape=[jax.ShapeDtypeStruct((t, d), F32), jax.ShapeDtypeStruct((d, t), BF16)],
        compiler_params=_params(1),
        name="merge_and_project",
    )(x, p, ya, yb,
      outs[0].reshape(t, C_OUT_WIDTH), by_residue(outs[1], d1), by_residue(outs[2], d2),
      lses[0].reshape(t, C_OUT_WIDTH), by_residue(lses[1], d1), by_residue(lses[2], d2),
      wa, wb, wc, wo, g_ffn.reshape(1, d), perm)


def _candidate_pairs():
    return [(a, b) for a in range(PEER_TOPK) for b in range(PEER_TOPK // (a + 1))]


CAND_ROWS = 64


SUBLANES = 8


def _sorting_network(n):
    pairs, p = [], 1
    while p < n:
        k = p
        while k >= 1:
            for j in range(k % p, n - k, 2 * k):
                for i in range(min(k, n - j - k)):
                    if (i + j) // (2 * p) == (i + j + k) // (2 * p):
                        pairs.append((i + j, i + j + k))
            k //= 2
        p *= 2
    return pairs


def _top16_desc(slabs):
    lst = list(slabs)
    for a, b in _sorting_network(len(lst)):
        lst[a], lst[b] = jnp.maximum(lst[a], lst[b]), jnp.minimum(lst[a], lst[b])
    out = []
    for k in range(PEER_TOPK):
        head = jnp.max(lst[0], axis=0, keepdims=True)
        out.append(head)
        if k == PEER_TOPK - 1:
            break
        taken = lst[0] == head
        keep = min(len(lst), PEER_TOPK - k - 1)
        lst = [jnp.where(taken, lst[a + 1] if a + 1 < len(lst) else -jnp.inf, lst[a])
               for a in range(keep)]
    return out


def _row_slabs(x):
    return [x[r:r + SUBLANES, :] for r in range(0, x.shape[0], SUBLANES)]


COUNT_SPLIT_B = 4
COUNT_SPLIT_A = 3
assert (COUNT_SPLIT_A + 1) * (COUNT_SPLIT_B + 1) > PEER_TOPK


def _fold_keys_kernel(k_ref, wq_ref, o_ref):
    o_ref[...] = lax.dot_general(k_ref[...], wq_ref[...], (((1,), (1,)), ((), ())),
                                 preferred_element_type=F32,
                                 precision=lax.Precision.HIGHEST).astype(o_ref.dtype)


def fold_sub_keys(w_q, layer, k1, k2):
    _, d, n = w_q.shape
    half = PEER_KEY_DIM // 2
    keys = jnp.stack([k1, k2])
    return pl.pallas_call(
        _fold_keys_kernel,
        grid=(n // half,),
        in_specs=[pl.BlockSpec((None, PEER_KEYS, half), lambda i: (i % 2, 0, 0)),
                  pl.BlockSpec((None, d, half), lambda i: (layer, 0, i))],
        out_specs=pl.BlockSpec((PEER_KEYS, d), lambda i: (i, 0)),
        out_shape=jax.ShapeDtypeStruct((n // half * PEER_KEYS, d), BF16),
        compiler_params=_params(1),
        name="fold_sub_keys",
    )(keys, w_q)


def _peer_select_kernel(h_ref, ws_ref, e2_ref, r2_ref, w_ref, n_ref, cand_ref, v2_ref):
    tm = h_ref.shape[1]
    scores = jnp.dot(ws_ref[...], h_ref[...], preferred_element_type=F32)
    cand_ref[...] = jnp.full(cand_ref.shape, -jnp.inf, F32)
    for hd in range(PEER_HEADS):
        s1 = scores[(2 * hd) * PEER_KEYS:(2 * hd + 1) * PEER_KEYS, :]
        s2 = scores[(2 * hd + 1) * PEER_KEYS:(2 * hd + 2) * PEER_KEYS, :]
        v1 = _top16_desc(_row_slabs(s1))
        v2 = _top16_desc(_row_slabs(s2))
        for b in range(PEER_TOPK):
            v2_ref[b:b + 1, :] = v2[b]
        for idx, (a, b) in enumerate(_candidate_pairs()):
            cand_ref[idx:idx + 1, :] = v1[a] + v2[b]
        best = _top16_desc(_row_slabs(cand_ref[...]))
        tau = best[PEER_TOPK - 1]
        z = functools.reduce(lambda acc, c: acc + jnp.exp(c - best[0]), best[1:],
                             jnp.ones_like(best[0]))
        rank2 = jnp.zeros(s2.shape, F32)
        for b in range(PEER_TOPK):
            rank2 = jnp.where(v2[b] > s2, float(b + 1), rank2)
        count = jnp.zeros(s1.shape, F32)
        for b in range(COUNT_SPLIT_B):
            count = jnp.where(s1 + v2[b] >= tau, float(b + 1), count)
        v2_all = v2_ref[...]
        low_rank = lax.broadcasted_iota(jnp.int32, v2_all.shape, 0) >= COUNT_SPLIT_B
        for a in range(COUNT_SPLIT_A):
            reached = jnp.logical_and(v1[a] + v2_all >= tau, low_rank)
            extra = jnp.sum(jnp.where(reached, 1.0, 0.0), axis=0, keepdims=True)
            count = count + jnp.where(s1 == v1[a], extra, 0.0)
        e2_ref[hd] = jnp.exp(s2 - v2[0]).astype(e2_ref.dtype)
        r2_ref[hd] = rank2.astype(r2_ref.dtype)
        w_ref[hd] = jnp.exp(s1 - v1[0]) / z
        n_ref[hd] = count


def peer_select(h_t, w_scores, tm=512):
    d, t = h_t.shape
    tab = lambda dt: jax.ShapeDtypeStruct((PEER_HEADS, PEER_KEYS, t), dt)
    blk = pl.BlockSpec((PEER_HEADS, PEER_KEYS, tm), lambda i: (0, 0, i))
    return pl.pallas_call(
        _peer_select_kernel,
        grid=(t // tm,),
        in_specs=[pl.BlockSpec((d, tm), lambda i: (0, i)),
                  _resident(w_scores.shape)],
        out_specs=[blk, blk, blk, blk],
        out_shape=[tab(BF16), tab(BF16), tab(F32), tab(F32)],
        scratch_shapes=[pltpu.VMEM((CAND_ROWS, tm), F32), pltpu.VMEM((PEER_TOPK, tm), F32)],
        compiler_params=_params(1),
        name="peer_select",
    )(h_t, w_scores)


def _peer_dense_kernel(h_ref, u_ref, vt_ref, e2_ref, r2_ref, w_ref, n_ref, x_ref, g_ref,
                       *rest, first_keys, final):
    if final:
        out_ref, acc_ref = rest
    else:
        perm_ref, xo_ref, hn_ref, *hd_refs, acc_ref = rest
    e = pl.program_id(1)
    tm = h_ref.shape[1]

    @pl.when(e == 0)
    def _():
        acc_ref[...] = jnp.zeros_like(acc_ref)

    first_key = e * first_keys
    gates = []
    for ii in range(first_keys):
        g = None
        for hd in range(PEER_HEADS):
            n_row = jnp.broadcast_to(n_ref[hd, pl.ds(first_key + ii, 1), :].astype(BF16), (PEER_KEYS, tm))
            w_row = jnp.broadcast_to(w_ref[hd, pl.ds(first_key + ii, 1), :].astype(BF16), (PEER_KEYS, tm))
            picked = jnp.where(r2_ref[hd] < n_row, e2_ref[hd], jnp.zeros((), BF16)) * w_row
            g = picked if g is None else g + picked
        gates.append(g)
    gate_t = jnp.concatenate(gates, axis=0)

    a_t = jnp.dot(u_ref[...], h_ref[...], preferred_element_type=F32)
    act_t = _gelu(a_t).astype(BF16) * gate_t
    acc_ref[...] += jnp.dot(vt_ref[...], act_t, preferred_element_type=F32)

    @pl.when(e == pl.num_programs(1) - 1)
    def _():
        x = x_ref[...] + acc_ref[...].T
        if final:
            out_ref[...] = _rms(x, g_ref[...])
        else:
            xo_ref[...] = x
            _write_norms(x, g_ref[...], perm_ref, hn_ref, hd_refs)


def peer_dense(h_t, u, v_tiles, e2, r2, w, n, x, g_next, final, batch, seq, tm=512):
    d, t = h_t.shape
    tiles = seq // tm
    n_tiles, _, te = v_tiles.shape
    first_keys = te // PEER_KEYS
    tab = pl.BlockSpec((PEER_HEADS, PEER_KEYS, tm), lambda i, e: (0, 0, i))
    operands = [h_t, u, v_tiles, e2, r2, w, n, x, g_next.reshape(1, d)]
    in_specs = [pl.BlockSpec((d, tm), lambda i, e: (0, i)),
                pl.BlockSpec((te, d), lambda i, e: (e, 0)),
                pl.BlockSpec((None, d, te), lambda i, e: (e, 0, 0)),
                tab, tab, tab, tab,
                pl.BlockSpec((tm, d), lambda i, e: (i, 0), pipeline_mode=pl.Buffered(1)),
                _resident((1, d))]
    once = pl.Buffered(1)
    tok_out = pl.BlockSpec((tm, d), lambda i, e: (i, 0), pipeline_mode=once)
    if final:
        out_specs = tok_out
        out_shape = jax.ShapeDtypeStruct((t, d), F32)
    else:
        specs, shapes = _norm_specs(batch, seq, d, tm, lambda i, e: (i // tiles, i % tiles), once)
        out_specs = [tok_out] + specs
        out_shape = [jax.ShapeDtypeStruct((t, d), F32)] + shapes
        perm = jnp.asarray(_group_by_residue_matrices(tm), BF16)
        operands.append(perm)
        in_specs.append(_resident(perm.shape))
    return pl.pallas_call(
        functools.partial(_peer_dense_kernel, first_keys=first_keys, final=final),
        grid=(t // tm, n_tiles),
        in_specs=in_specs,
        out_specs=out_specs,
        out_shape=out_shape,
        scratch_shapes=[pltpu.VMEM((d, tm), F32)],
        compiler_params=_params(2),
        name="peer_dense_final" if final else "peer_dense",
    )(*operands)


W_IN_GATE_COL = IN_WIDTH - GATE_WIDTH
W_IN_Q_COL = 2 * A_WIDTH + B_WIDTH


def _main_weight_block(j):
    gate_blocks = GATE_WIDTH // QKV_GROUP_WIDTH
    return jnp.where(j < gate_blocks, j + W_IN_GATE_COL // QKV_GROUP_WIDTH, j - gate_blocks)


def _qkv_weights(w, g):
    cols = [W_IN_Q_COL + part * C_QKV_WIDTH + g * C_OUT_WIDTH for part in range(3)]
    return jnp.concatenate([w[:, c:c + C_OUT_WIDTH] for c in cols], axis=1).astype(BF16)


def kernel(x, rel_bias_table, norm_mix_g, w_in, a_norm_g, a_w_s, a_b_s, b_w_pool, b_scale, w_br_a,
           w_br_b, w_br_c, w_out, norm_ffn_g, peer_w_q, peer_sub_keys_1, peer_sub_keys_2, peer_u,
           peer_v, final_norm_g):
    batch, seq, d = x.shape
    t = batch * seq
    xf = x.reshape(t, d)
    hs = rmsnorm_rows(xf, norm_mix_g[0], batch, seq)
    out = None
    for l in range(DEPTH):
        p = in_proj_f32_weights(hs[0], w_in, l, MAIN_WIDTH, _main_weight_block)
        ya = mixer_a(p, a_norm_g[l], a_w_s[l], a_b_s[l])
        yb = mixer_b(p, b_w_pool[l].astype(BF16), b_scale[l], seq)
        outs, lses = [], []
        for g in range(C_GROUPS):
            qkv = in_proj(hs[g].reshape(t, d), _qkv_weights(w_in[l], g))
            o, lse = dilated_attention_group(qkv, 0, rel_bias_table, g, batch, seq)
            outs.append(o)
            lses.append(lse)
        x_mid, h_ffn = merge_and_project(
            xf, p, ya, yb, outs, lses, w_br_a[l].astype(BF16), w_br_b[l].astype(BF16),
            w_br_c[l].astype(BF16), w_out[l].astype(BF16), norm_ffn_g[l], batch, seq)
        w_scores = fold_sub_keys(peer_w_q, l, peer_sub_keys_1[l], peer_sub_keys_2[l])
        e2, r2, w, n = peer_select(h_ffn, w_scores)
        final = l == DEPTH - 1
        g_next = final_norm_g if final else norm_mix_g[l + 1]
        res = peer_dense(h_ffn, to_bf16(peer_u, l, PEER_EXPERT_TILE),
                         transposed_tiles_bf16(peer_v, l, PEER_EXPERT_TILE), e2, r2, w, n,
                         x_mid, g_next, final, batch, seq)
        if final:
            out = res
        else:
            xf, hs = res[0], res[1:]
    return out.reshape(batch, seq, d)
```
